```python
import math
import jax, jax.numpy as jnp
from jax import lax
import numpy as np

D_MODEL = 1024
BATCH = 1
SEQ = 16384
DEPTH = 1
DEC_BATCH = 8
DEC_SEQ = 8192
PAST_LEN = 128

MIX_WIDTH = D_MODEL
ATTN_WIDTH = D_MODEL // 2
ATTN_HEAD_DIM = 64
ATTN_HEADS = ATTN_WIDTH // ATTN_HEAD_DIM
ATTN_KV_HEADS = 2
ATTN_GROUP = ATTN_HEADS // ATTN_KV_HEADS
KV_DIM = ATTN_KV_HEADS * ATTN_HEAD_DIM
WINDOW = 128
BLOCK = 128
DN_WIDTH = MIX_WIDTH - ATTN_WIDTH
DN_HEAD_DIM = 128
DN_HEADS = DN_WIDTH // DN_HEAD_DIM
CONV_WIDTH = 5
CONV_PAD = CONV_WIDTH // 2
CHUNK = 64
D_FF = 2816
ALPHA = (2.0 * DEPTH) ** 0.25
BETA_INIT = (8.0 * DEPTH) ** -0.25
LN_EPS = 1e-5
RMS_EPS = 1e-6

OFF_AK = ATTN_WIDTH
OFF_AV = OFF_AK + KV_DIM
OFF_DQ = OFF_AV + KV_DIM
OFF_DK = OFF_DQ + DN_WIDTH
OFF_DV = OFF_DK + DN_WIDTH
OFF_Z = OFF_DV + DN_WIDTH
OFF_B = OFF_Z + DN_WIDTH
OFF_A = OFF_B + 2 * DN_HEADS
PROJ_DIM = OFF_A + 2 * DN_HEADS
SPLIT_POINTS = (OFF_AK, OFF_AV, OFF_DQ, OFF_DK, OFF_DV, OFF_Z, OFF_B, OFF_A)

kernel_name = "hymba_swa_gdn_macaron_deepnorm_encoder"


def layer_norm(x, gain, bias):
    xf = x.astype(jnp.float32)
    mu = jnp.mean(xf, axis=-1, keepdims=True)
    var = jnp.mean(jnp.square(xf - mu), axis=-1, keepdims=True)
    y = (xf - mu) * lax.rsqrt(var + LN_EPS) * gain.astype(jnp.float32) + bias.astype(jnp.float32)
    return y.astype(x.dtype)


def swiglu(x, w_in, w_out):
    gate, up = jnp.split(x @ w_in, 2, axis=-1)
    return (jax.nn.silu(gate) * up) @ w_out


def l2norm(t):
    return t * lax.rsqrt(jnp.sum(t * t, axis=-1, keepdims=True) + RMS_EPS)


def windowed_gqa_attention(q, k, v, sink):
    b, l, _ = q.shape
    nb = l // BLOCK
    qb = q.reshape(b, nb, BLOCK, ATTN_KV_HEADS, ATTN_GROUP, ATTN_HEAD_DIM)

    def band(t):
        tp = jnp.pad(t, ((0, 0), (BLOCK, BLOCK), (0, 0)))
        tp = tp.reshape(b, nb + 2, BLOCK, ATTN_KV_HEADS, ATTN_HEAD_DIM)
        return jnp.concatenate([tp[:, :-2], tp[:, 1:-1], tp[:, 2:]], axis=2)

    kb, vb = band(k), band(v)
    scores = jnp.einsum('bnqhgd,bnkhd->bnhgqk', qb, kb,
                        preferred_element_type=jnp.float32) * (ATTN_HEAD_DIM ** -0.5)
    qi = jnp.arange(BLOCK)[:, None]
    kj = jnp.arange(3 * BLOCK)[None, :]
    dist = jnp.abs(qi - kj + BLOCK)
    in_win = dist <= WINDOW
    s_abs = (jnp.arange(nb)[:, None] - 1) * BLOCK + jnp.arange(3 * BLOCK)[None, :]
    in_seq = (s_abs >= 0) & (s_abs < l)
    mask = in_win[None, :, :] & in_seq[:, None, :]
    slopes = 2.0 ** (-8.0 * jnp.arange(1, ATTN_HEADS + 1, dtype=jnp.float32) / ATTN_HEADS)
    alibi = (-slopes[:, None, None] * dist.astype(jnp.float32)[None]).reshape(
        ATTN_KV_HEADS, ATTN_GROUP, BLOCK, 3 * BLOCK)
    logits = jnp.where(mask[None, :, None, None], scores + alibi[None, None], -jnp.inf)
    sink_l = sink.astype(jnp.float32).reshape(ATTN_KV_HEADS, ATTN_GROUP)[None, None, :, :, None, None]
    m = jnp.maximum(jnp.max(logits, axis=-1, keepdims=True), sink_l)
    p = jnp.exp(logits - m)
    probs = p / (jnp.sum(p, axis=-1, keepdims=True) + jnp.exp(sink_l - m))
    out = jnp.einsum('bnhgqk,bnkhd->bnqhgd', probs.astype(vb.dtype), vb)
    return out.reshape(b, l, ATTN_WIDTH)


def short_conv(x, w):
    c = x.shape[-1]
    return lax.conv_general_dilated(x, w[:, None, :], window_strides=(1,),
                                    padding=[(CONV_PAD, CONV_PAD)],
                                    dimension_numbers=('NWC', 'WIO', 'NWC'),
                                    feature_group_count=c)


def gated_delta_chunked(q, k, v, g, beta):
    b, h, l, dk = q.shape
    dv = v.shape[-1]
    nc = l // CHUNK
    q = q.reshape(b, h, nc, CHUNK, dk)
    k = k.reshape(b, h, nc, CHUNK, dk)
    v = v.reshape(b, h, nc, CHUNK, dv)
    gc = jnp.cumsum(g.reshape(b, h, nc, CHUNK), axis=-1)
    beta = beta.reshape(b, h, nc, CHUNK)
    idx = jnp.arange(CHUNK)
    incl = idx[:, None] >= idx[None, :]
    strict = idx[:, None] > idx[None, :]
    diff = gc[..., :, None] - gc[..., None, :]
    decay = jnp.where(incl, jnp.exp(jnp.where(incl, diff, 0.0)), 0.0)
    k_beta = k * beta[..., None]
    m = jnp.where(strict, jnp.einsum('bhnid,bhnjd->bhnij', k_beta, k) * decay, 0.0)
    a = m + jnp.eye(CHUNK, dtype=m.dtype)
    rhs = jnp.concatenate([v * beta[..., None], k_beta * jnp.exp(gc)[..., None]], axis=-1)
    sol = lax.linalg.triangular_solve(a, rhs, left_side=True, lower=True, unit_diagonal=True)
    u, w = sol[..., :dv], sol[..., dv:]
    attn = jnp.where(incl, jnp.einsum('bhnid,bhnjd->bhnij', q, k) * decay, 0.0)
    q_dec = q * jnp.exp(gc)[..., None]
    k_dec = k * jnp.exp(gc[..., -1:] - gc)[..., None]
    g_last = jnp.exp(gc[..., -1])

    def step(state, xs):
        q_c, k_c, u_c, w_c, attn_c, gl_c = xs
        v_new = u_c - jnp.einsum('bhcd,bhde->bhce', w_c, state)
        o_c = (jnp.einsum('bhcd,bhde->bhce', q_c, state)
               + jnp.einsum('bhij,bhje->bhie', attn_c, v_new))
        state = state * gl_c[..., None, None] + jnp.einsum('bhcd,bhce->bhde', k_c, v_new)
        return state, o_c

    xs = tuple(jnp.moveaxis(t, 2, 0) for t in (q_dec, k_dec, u, w, attn, g_last))
    s0 = jnp.zeros((b, h, dk, dv), jnp.float32)
    _, o = lax.scan(step, s0, xs)
    return jnp.moveaxis(o, 0, 2).reshape(b, h, l, dv)


def hybrid_mixer(x, w_in, conv_w, sink, a_log, dt_bias, norm_gain, w_out):
    b, l, _ = x.shape
    proj = x @ w_in
    aq, ak, av, dq, dk_, dv_, z, bb, aa = jnp.split(proj, SPLIT_POINTS, axis=-1)
    o_attn = windowed_gqa_attention(aq, ak, av, sink)
    qkv = jax.nn.silu(short_conv(jnp.concatenate([dq, dk_, dv_], axis=-1), conv_w))
    dq, dk_, dv_ = jnp.split(qkv, 3, axis=-1)

    def heads(t):
        return t.reshape(b, l, DN_HEADS, DN_HEAD_DIM).transpose(0, 2, 1, 3).astype(jnp.float32)

    q = l2norm(heads(dq)) * (DN_HEAD_DIM ** -0.5)
    k = l2norm(heads(dk_))
    v = heads(dv_)
    beta = jax.nn.sigmoid(bb.astype(jnp.float32)).reshape(b, l, 2, DN_HEADS).transpose(2, 0, 3, 1)
    a_in = aa.astype(jnp.float32).reshape(b, l, 2, DN_HEADS).transpose(2, 0, 3, 1)
    g = -jnp.exp(a_log.astype(jnp.float32))[:, None, :, None] * jax.nn.softplus(
        a_in + dt_bias.astype(jnp.float32)[:, None, :, None])
    o_fwd = gated_delta_chunked(q, k, v, g[0], beta[0])
    o_bwd = jnp.flip(gated_delta_chunked(jnp.flip(q, 2), jnp.flip(k, 2), jnp.flip(v, 2),
                                         jnp.flip(g[1], 2), jnp.flip(beta[1], 2)), 2)
    o = (o_fwd + o_bwd).transpose(0, 2, 1, 3)
    zg = z.astype(jnp.float32).reshape(b, l, DN_HEADS, DN_HEAD_DIM)
    o = (o * lax.rsqrt(jnp.mean(o * o, axis=-1, keepdims=True) + RMS_EPS)
         * norm_gain.astype(jnp.float32) * jax.nn.silu(zg))
    o_dn = o.reshape(b, l, DN_WIDTH).astype(x.dtype)
    return jnp.concatenate([o_attn, o_dn], axis=-1) @ w_out


def setup_inputs(seed: int = 0) -> dict:
    key = jax.random.key(seed)
    ks = jax.random.split(key, 16)
    f32 = jnp.float32
    x_prompt = jax.random.normal(ks[0], (BATCH, SEQ, D_MODEL), f32)
    x_sample = jax.random.normal(ks[1], (DEC_BATCH, DEC_SEQ, D_MODEL), f32)
    ffn1_w_in = jax.random.normal(ks[2], (DEPTH, D_MODEL, 2 * D_FF), f32) * (D_MODEL ** -0.5) * BETA_INIT
    ffn1_w_out = jax.random.normal(ks[3], (DEPTH, D_FF, D_MODEL), f32) * (D_FF ** -0.5) * BETA_INIT
    col_scale = jnp.concatenate([
        jnp.ones((OFF_AV,), f32), jnp.full((KV_DIM,), BETA_INIT, f32),
        jnp.ones((OFF_DV - OFF_DQ,), f32), jnp.full((DN_WIDTH,), BETA_INIT, f32),
        jnp.ones((PROJ_DIM - OFF_Z,), f32)])
    w_in = jax.random.normal(ks[4], (DEPTH, D_MODEL, PROJ_DIM), f32) * (D_MODEL ** -0.5) * col_scale
    conv_w = jax.random.normal(ks[5], (DEPTH, CONV_WIDTH, 3 * DN_WIDTH), f32) * (CONV_WIDTH ** -0.5)
    attn_sink = jax.random.normal(ks[6], (DEPTH, ATTN_HEADS), f32) * 0.5
    dn_a_log = jnp.log(jax.random.uniform(ks[7], (DEPTH, 2, DN_HEADS), f32, 1.0, 16.0))
    dt = jnp.exp(jax.random.uniform(ks[8], (DEPTH, 2, DN_HEADS), f32, math.log(1e-3), math.log(1e-1)))
    dn_dt_bias = dt + jnp.log(-jnp.expm1(-dt))
    dn_norm_gain = 1.0 + 0.02 * jax.random.normal(ks[9], (DEPTH, DN_HEAD_DIM), f32)
    w_out = jax.random.normal(ks[10], (DEPTH, MIX_WIDTH, D_MODEL), f32) * (MIX_WIDTH ** -0.5) * BETA_INIT
    ffn2_w_in = jax.random.normal(ks[11], (DEPTH, D_MODEL, 2 * D_FF), f32) * (D_MODEL ** -0.5) * BETA_INIT
    ffn2_w_out = jax.random.normal(ks[12], (DEPTH, D_FF, D_MODEL), f32) * (D_FF ** -0.5) * BETA_INIT
    ln_gain = 1.0 + 0.02 * jax.random.normal(ks[13], (DEPTH, 3, D_MODEL), f32)
    ln_bias = 0.02 * jax.random.normal(ks[14], (DEPTH, 3, D_MODEL), f32)
    return {"x_prompt": x_prompt, "x_sample": x_sample,
            "ffn1_w_in": ffn1_w_in, "ffn1_w_out": ffn1_w_out,
            "w_in": w_in, "conv_w": conv_w, "attn_sink": attn_sink,
            "dn_a_log": dn_a_log, "dn_dt_bias": dn_dt_bias, "dn_norm_gain": dn_norm_gain,
            "w_out": w_out, "ffn2_w_in": ffn2_w_in, "ffn2_w_out": ffn2_w_out,
            "ln_gain": ln_gain, "ln_bias": ln_bias}


def reference(x_prompt, x_sample, ffn1_w_in, ffn1_w_out, w_in, conv_w, attn_sink,
              dn_a_log, dn_dt_bias, dn_norm_gain, w_out, ffn2_w_in, ffn2_w_out,
              ln_gain, ln_bias):
    def trunk(x):
        for i in range(DEPTH):
            x = layer_norm(ALPHA * x + 0.5 * swiglu(x, ffn1_w_in[i], ffn1_w_out[i]),
                           ln_gain[i, 0], ln_bias[i, 0])
            x = layer_norm(ALPHA * x + hybrid_mixer(x, w_in[i], conv_w[i], attn_sink[i],
                                                   dn_a_log[i], dn_dt_bias[i],
                                                   dn_norm_gain[i], w_out[i]),
                           ln_gain[i, 1], ln_bias[i, 1])
            x = layer_norm(ALPHA * x + 0.5 * swiglu(x, ffn2_w_in[i], ffn2_w_out[i]),
                           ln_gain[i, 2], ln_bias[i, 2])
        return x

    y_prompt = trunk(x_prompt)
    y_sample = trunk(x_sample)
    return (y_prompt, y_sample)
```

```python
import functools

import jax
import jax.numpy as jnp
from jax import lax
from jax.experimental import pallas as pl
from jax.experimental.pallas import tpu as pltpu

F32 = jnp.float32
BF16 = jnp.bfloat16

D_MODEL = 1024
ATTN_WIDTH = 512
ATTN_HEAD_DIM = 64
ATTN_HEADS = 8
ATTN_KV_HEADS = 2
ATTN_GROUP = ATTN_HEADS // ATTN_KV_HEADS
KV_DIM = ATTN_KV_HEADS * ATTN_HEAD_DIM
WINDOW = 128
BLOCK = 128
DN_WIDTH = 512
DN_HEAD_DIM = 128
DN_HEADS = DN_WIDTH // DN_HEAD_DIM
CONV_WIDTH = 5
CONV_PAD = CONV_WIDTH // 2
CHUNK = 64
D_FF = 2816
LN_EPS = 1e-5
RMS_EPS = 1e-6
OFF_B = ATTN_WIDTH + 2 * KV_DIM + 4 * DN_WIDTH
OFF_A = OFF_B + 2 * DN_HEADS

V7X_LANES = 128
V7X_BF16_SUBLANES = 16
V7X_VMEM_BYTES = 64 * 1024 * 1024

ROW_TILE = 512
FFN_CHUNK = 256
PREP_TILE = 256
SCAN_BLOCK = 2 * CHUNK
PROJ_PAD = V7X_LANES
NEG_BIG = -1e30


def _vmem_limit(nbytes):
    return int(min(nbytes, V7X_VMEM_BYTES - 4 * 1024 * 1024))


def _layer_norm(r, gain, bias):
    mu = jnp.mean(r, axis=-1, keepdims=True)
    c = r - mu
    var = jnp.mean(c * c, axis=-1, keepdims=True)
    return c * lax.rsqrt(var + LN_EPS) * gain + bias


def _silu(t):
    return t * jax.nn.sigmoid(t)


def _dot(a, b):
    return jnp.dot(a, b, preferred_element_type=F32)


def _ffn_ln_kernel(alpha, x_ref, win_ref, wout_ref, g_ref, b_ref, o_ref, act_ref):
    x = x_ref[...]
    xb = x.astype(BF16)
    for j in range(D_FF // FFN_CHUNK):
        lo = j * FFN_CHUNK
        gate = _dot(xb, win_ref[:, lo:lo + FFN_CHUNK])
        up = _dot(xb, win_ref[:, D_FF + lo:D_FF + lo + FFN_CHUNK])
        act_ref[:, lo:lo + FFN_CHUNK] = (_silu(gate) * up).astype(BF16)
    y = _dot(act_ref[...], wout_ref[...])
    o_ref[...] = _layer_norm(alpha * x + 0.5 * y, g_ref[...], b_ref[...])


def _ffn_ln(x2, w_in, w_out, gain, bias, alpha):
    rows = x2.shape[0]
    const = lambda i: (0, 0)
    return pl.pallas_call(
        functools.partial(_ffn_ln_kernel, alpha),
        out_shape=jax.ShapeDtypeStruct((rows, D_MODEL), F32),
        grid=(rows // ROW_TILE,),
        in_specs=[
            pl.BlockSpec((ROW_TILE, D_MODEL), lambda i: (i, 0)),
            pl.BlockSpec((D_MODEL, 2 * D_FF), const, pipeline_mode=pl.Buffered(1)),
            pl.BlockSpec((D_FF, D_MODEL), const, pipeline_mode=pl.Buffered(1)),
            pl.BlockSpec((1, D_MODEL), const),
            pl.BlockSpec((1, D_MODEL), const),
        ],
        out_specs=pl.BlockSpec((ROW_TILE, D_MODEL), lambda i: (i, 0)),
        scratch_shapes=[pltpu.VMEM((ROW_TILE, D_FF), BF16)],
        compiler_params=pltpu.CompilerParams(
            dimension_semantics=("arbitrary",), vmem_limit_bytes=_vmem_limit(48 << 20)),
        name="ffn_ln",
    )(x2, w_in, w_out, gain, bias)


_PROJ_SPLITS = (
    ("aq", 0, ATTN_WIDTH, BF16),
    ("ak", ATTN_WIDTH, KV_DIM, BF16),
    ("av", ATTN_WIDTH + KV_DIM, KV_DIM, BF16),
    ("dqkv", ATTN_WIDTH + 2 * KV_DIM, 3 * DN_WIDTH, BF16),
    ("z", ATTN_WIDTH + 2 * KV_DIM + 3 * DN_WIDTH, DN_WIDTH, BF16),
    ("bb", OFF_B, PROJ_PAD, F32),
    ("aa", OFF_B + PROJ_PAD, PROJ_PAD, F32),
)
_PROJ_COLS = OFF_B + 2 * PROJ_PAD


def _in_proj_kernel(x_ref, w_ref, *out_refs):
    xb = x_ref[...].astype(BF16)
    for (_, lo, width, dtype), o_ref in zip(_PROJ_SPLITS, out_refs):
        step = min(width, ATTN_WIDTH)
        for c in range(0, width, step):
            o_ref[:, c:c + step] = _dot(xb, w_ref[:, lo + c:lo + c + step]).astype(dtype)


def _in_proj(x2, w_all):
    rows = x2.shape[0]
    return pl.pallas_call(
        _in_proj_kernel,
        out_shape=[jax.ShapeDtypeStruct((rows, width), dtype) for _, _, width, dtype in _PROJ_SPLITS],
        grid=(rows // ROW_TILE,),
        in_specs=[
            pl.BlockSpec((ROW_TILE, D_MODEL), lambda i: (i, 0)),
            pl.BlockSpec((D_MODEL, _PROJ_COLS), lambda i: (0, 0), pipeline_mode=pl.Buffered(1)),
        ],
        out_specs=[pl.BlockSpec((ROW_TILE, width), lambda i: (i, 0)) for _, _, width, _ in _PROJ_SPLITS],
        compiler_params=pltpu.CompilerParams(
            dimension_semantics=("arbitrary",), vmem_limit_bytes=_vmem_limit(40 << 20)),
        name="in_proj",
    )(x2, w_all)


def _attn_kernel(q_ref, kp_ref, kc_ref, kn_ref, vp_ref, vc_ref, vn_ref, bias_ref, sink_ref, o_ref):
    n = pl.program_id(1)
    last = pl.num_programs(1) - 1
    q = q_ref[0]
    kcat = jnp.concatenate([kp_ref[0], kc_ref[0], kn_ref[0]], axis=0).astype(F32)
    vcat = jnp.concatenate([vp_ref[0], vc_ref[0], vn_ref[0]], axis=0).astype(F32)
    krot = pltpu.roll(kcat, ATTN_HEAD_DIM, axis=1)
    vrot = pltpu.roll(vcat, ATTN_HEAD_DIM, axis=1)
    low_half = lax.broadcasted_iota(jnp.int32, kcat.shape, 1) < ATTN_HEAD_DIM
    group_of_lane = lax.broadcasted_iota(jnp.int32, (BLOCK, ATTN_GROUP * ATTN_HEAD_DIM), 1) // ATTN_HEAD_DIM
    key_pos = lax.broadcasted_iota(jnp.int32, (1, 3 * BLOCK), 1)
    outside = ((key_pos < BLOCK) & (n == 0)) | ((key_pos >= 2 * BLOCK) & (n == last))
    scale = ATTN_HEAD_DIM ** -0.5
    outs = []
    for h in range(ATTN_KV_HEADS):
        k_pair = jnp.where(low_half, kcat, krot) if h == 0 else jnp.where(low_half, krot, kcat)
        v_pair = jnp.where(low_half, vcat, vrot) if h == 0 else jnp.where(low_half, vrot, vcat)
        k_rep = jnp.concatenate([k_pair, k_pair], axis=1).astype(BF16)
        v_rep = jnp.concatenate([v_pair, v_pair], axis=1).astype(BF16)
        qh = q[:, h * 256:(h + 1) * 256]
        q_stack = jnp.concatenate(
            [jnp.where(group_of_lane == g, qh, jnp.zeros_like(qh)) for g in range(ATTN_GROUP)], axis=0)
        s = lax.dot_general(q_stack, k_rep, (((1,), (1,)), ((), ())), preferred_element_type=F32)
        logits = s * scale + bias_ref[h]
        logits = jnp.where(outside, NEG_BIG, logits)
        sink = sink_ref[h][:, :1]
        m = jnp.maximum(jnp.max(logits, axis=-1, keepdims=True), sink)
        p = jnp.exp(logits - m)
        denom = jnp.sum(p, axis=-1, keepdims=True) + jnp.exp(sink - m)
        pv = _dot(p.astype(BF16), v_rep) / denom
        out_h = jnp.zeros((BLOCK, ATTN_GROUP * ATTN_HEAD_DIM), F32)
        for g in range(ATTN_GROUP):
            out_h = out_h + jnp.where(group_of_lane == g, pv[g * BLOCK:(g + 1) * BLOCK], 0.0)
        outs.append(out_h)
    o_ref[0] = jnp.concatenate(outs, axis=1).astype(BF16)


def _attention(q, k, v, bias, sink):
    b, l, _ = q.shape
    nb = l // BLOCK
    cur = lambda bi, n: (bi, n, 0)
    prev = lambda bi, n: (bi, jnp.maximum(n - 1, 0), 0)
    nxt = lambda bi, n: (bi, jnp.minimum(n + 1, nb - 1), 0)
    const = lambda bi, n: (0, 0, 0)
    kv_spec = lambda imap: pl.BlockSpec((1, BLOCK, KV_DIM), imap)
    return pl.pallas_call(
        _attn_kernel,
        out_shape=jax.ShapeDtypeStruct((b, l, ATTN_WIDTH), BF16),
        grid=(b, nb),
        in_specs=[
            pl.BlockSpec((1, BLOCK, ATTN_WIDTH), cur),
            kv_spec(prev), kv_spec(cur), kv_spec(nxt),
            kv_spec(prev), kv_spec(cur), kv_spec(nxt),
            pl.BlockSpec((ATTN_KV_HEADS, ATTN_GROUP * BLOCK, 3 * BLOCK), const),
            pl.BlockSpec((ATTN_KV_HEADS, ATTN_GROUP * BLOCK, V7X_LANES), const),
        ],
        out_specs=pl.BlockSpec((1, BLOCK, ATTN_WIDTH), cur),
        compiler_params=pltpu.CompilerParams(
            dimension_semantics=("arbitrary", "arbitrary"), vmem_limit_bytes=_vmem_limit(32 << 20)),
        name="swa_attention",
    )(q, k, k, k, v, v, v, bias, sink)


def _attn_tables(attn_sink):
    slopes = 2.0 ** (-8.0 * jnp.arange(1, ATTN_HEADS + 1, dtype=F32) / ATTN_HEADS)
    qi = jnp.arange(BLOCK)[:, None]
    kj = jnp.arange(3 * BLOCK)[None, :]
    dist = jnp.abs(qi - kj + BLOCK)
    bias = jnp.where(dist <= WINDOW, -slopes[:, None, None] * dist.astype(F32)[None], NEG_BIG)
    bias = bias.reshape(ATTN_KV_HEADS, ATTN_GROUP * BLOCK, 3 * BLOCK)
    sink = jnp.broadcast_to(attn_sink.astype(F32)[:, None, None], (ATTN_HEADS, BLOCK, V7X_LANES))
    return bias, sink.reshape(ATTN_KV_HEADS, ATTN_GROUP * BLOCK, V7X_LANES)


def _dn_prep_kernel(x_ref, xp_ref, xn_ref, w_ref, bb_ref, aa_ref, gp_ref, tri_ref,
                    q_ref, k_ref, v_ref, beta_ref, gc_ref, gct_ref):
    i = pl.program_id(1)
    last = pl.num_programs(1) - 1
    halo = V7X_BF16_SUBLANES
    prev = jnp.where(i > 0, xp_ref[0].astype(F32), 0.0)
    nxt = jnp.where(i < last, xn_ref[0].astype(F32), 0.0)
    out_refs = (q_ref, k_ref, v_ref)
    for c in range(3 * DN_HEADS):
        kind, h = divmod(c, DN_HEADS)
        sl = slice(c * DN_HEAD_DIM, (c + 1) * DN_HEAD_DIM)
        xe = jnp.concatenate([prev[:, sl], x_ref[0, :, sl].astype(F32), nxt[:, sl]], axis=0)
        rows = xe.shape[0]
        y = jnp.zeros((PREP_TILE, DN_HEAD_DIM), F32)
        for t in range(CONV_WIDTH):
            shift = (CONV_PAD - t) % rows
            xs = pltpu.roll(xe, shift, axis=0) if shift else xe
            y = y + w_ref[t:t + 1, sl] * xs[halo:halo + PREP_TILE]
        s = _silu(y)
        if kind < 2:
            s = s * lax.rsqrt(jnp.sum(s * s, axis=-1, keepdims=True) + RMS_EPS)
        if kind == 0:
            s = s * (DN_HEAD_DIM ** -0.5)
        out_refs[kind][0, h] = s.astype(BF16)

    beta_ref[0] = jax.nn.sigmoid(bb_ref[0])
    t = aa_ref[0] + gp_ref[1:2, :]
    softplus = jnp.maximum(t, 0.0) + jnp.log1p(jnp.exp(-jnp.abs(t)))
    g = -jnp.exp(gp_ref[0:1, :]) * softplus
    prefix = jnp.dot(tri_ref[0], g, preferred_element_type=F32, precision=lax.Precision.HIGHEST)
    suffix = jnp.dot(tri_ref[1], g, preferred_element_type=F32, precision=lax.Precision.HIGHEST)
    lane = lax.broadcasted_iota(jnp.int32, g.shape, 1)
    gc = jnp.where(lane < DN_HEADS, prefix, suffix)
    gc_ref[0] = gc
    for r in range(PREP_TILE // SCAN_BLOCK):
        gct_ref[0, r] = gc[r * SCAN_BLOCK:(r + 1) * SCAN_BLOCK, :].T[0:2 * DN_HEADS, :]


def _dn_prep(dqkv, bb, aa, conv_w, gate_params, tri):
    b, l, _ = dqkv.shape
    halo = V7X_BF16_SUBLANES
    per = PREP_TILE // halo
    nh = l // halo
    cur = lambda bi, i: (bi, i, 0)
    const2 = lambda bi, i: (0, 0)
    head_major = jax.ShapeDtypeStruct((b, DN_HEADS, l, DN_HEAD_DIM), BF16)
    gate = jax.ShapeDtypeStruct((b, l, PROJ_PAD), F32)
    return pl.pallas_call(
        _dn_prep_kernel,
        out_shape=[head_major, head_major, head_major, gate, gate,
                   jax.ShapeDtypeStruct((b, l // SCAN_BLOCK, 2 * DN_HEADS, SCAN_BLOCK), F32)],
        grid=(b, l // PREP_TILE),
        in_specs=[
            pl.BlockSpec((1, PREP_TILE, 3 * DN_WIDTH), cur),
            pl.BlockSpec((1, halo, 3 * DN_WIDTH), lambda bi, i: (bi, jnp.maximum(i * per - 1, 0), 0)),
            pl.BlockSpec((1, halo, 3 * DN_WIDTH), lambda bi, i: (bi, jnp.minimum((i + 1) * per, nh - 1), 0)),
            pl.BlockSpec((8, 3 * DN_WIDTH), const2),
            pl.BlockSpec((1, PREP_TILE, PROJ_PAD), cur),
            pl.BlockSpec((1, PREP_TILE, PROJ_PAD), cur),
            pl.BlockSpec((8, PROJ_PAD), const2),
            pl.BlockSpec((2, PREP_TILE, PREP_TILE), lambda bi, i: (0, 0, 0)),
        ],
        out_specs=[
            pl.BlockSpec((1, DN_HEADS, PREP_TILE, DN_HEAD_DIM), lambda bi, i: (bi, 0, i, 0)),
            pl.BlockSpec((1, DN_HEADS, PREP_TILE, DN_HEAD_DIM), lambda bi, i: (bi, 0, i, 0)),
            pl.BlockSpec((1, DN_HEADS, PREP_TILE, DN_HEAD_DIM), lambda bi, i: (bi, 0, i, 0)),
            pl.BlockSpec((1, PREP_TILE, PROJ_PAD), cur),
            pl.BlockSpec((1, PREP_TILE, PROJ_PAD), cur),
            pl.BlockSpec((1, PREP_TILE // SCAN_BLOCK, 2 * DN_HEADS, SCAN_BLOCK), lambda bi, i: (bi, i, 0, 0)),
        ],
        compiler_params=pltpu.CompilerParams(
            dimension_semantics=("arbitrary", "arbitrary"), vmem_limit_bytes=_vmem_limit(32 << 20)),
        name="dn_prep",
    )(dqkv, dqkv, dqkv, conv_w, bb, aa, gate_params, tri)


def _chunk_triangles():
    t = jnp.arange(PREP_TILE)
    same = (t[:, None] // CHUNK) == (t[None, :] // CHUNK)
    prefix = same & (t[None, :] <= t[:, None])
    suffix = same & (t[None, :] >= t[:, None])
    return jnp.stack([prefix, suffix]).astype(F32)


def _dn_scan_kernel(qf_ref, kf_ref, vf_ref, qb_ref, kb_ref, vb_ref,
                    betaf_ref, gcf_ref, gctf_ref, betab_ref, gcb_ref, gctb_ref,
                    of_ref, ob_ref, state_ref):
    @pl.when(pl.program_id(1) == 0)
    def _():
        state_ref[...] = jnp.zeros_like(state_ref)

    n = SCAN_BLOCK
    row = lax.broadcasted_iota(jnp.int32, (n, n), 0)
    col = lax.broadcasted_iota(jnp.int32, (n, n), 1)
    same_chunk = (row // CHUNK) == (col // CHUNK)
    eye = (row == col).astype(F32)
    off_diag = row != col
    directions = (
        (qf_ref, kf_ref, vf_ref, betaf_ref, gcf_ref, gctf_ref, of_ref, same_chunk & (row >= col)),
        (qb_ref, kb_ref, vb_ref, betab_ref, gcb_ref, gctb_ref, ob_ref, same_chunk & (row <= col)),
    )
    for d, (q_ref, k_ref, v_ref, beta_ref, gc_ref, gct_ref, o_ref, incl) in enumerate(directions):
        for h in range(DN_HEADS):
            lane = d * DN_HEADS + h
            kk = k_ref[0, h]
            qq = q_ref[0, h]
            vv = v_ref[0, h]
            beta = beta_ref[0][:, lane:lane + 1]
            gcol = gc_ref[0][:, lane:lane + 1]
            grow = gct_ref[0, 0][lane:lane + 1, :]
            gram = lax.dot_general(jnp.concatenate([kk, qq], axis=0), kk,
                                   (((1,), (1,)), ((), ())), preferred_element_type=F32)
            decay = jnp.exp(jnp.where(incl, gcol - grow, NEG_BIG))
            neg_n = -(beta * gram[:n] * jnp.where(off_diag, decay, 0.0))
            attn = gram[n:] * decay
            p = eye + neg_n
            zb = neg_n.astype(BF16)
            y = _dot(zb, zb)
            for _ in range(4):
                yb = y.astype(BF16)
                r = _dot(yb, jnp.concatenate([p.astype(BF16), yb], axis=1))
                p = p + r[:, :n]
                y = r[:, n:]
            t_inv = p + _dot(y.astype(BF16), p.astype(BF16))
            egc = jnp.exp(gcol)
            rhs = jnp.concatenate([vv.astype(F32) * beta, kk.astype(F32) * (beta * egc)], axis=1)
            uw = _dot(t_inv.astype(BF16), rhs.astype(BF16))
            uwb = uw.astype(BF16)
            auw = _dot(attn.astype(BF16), uwb)
            q_eff = qq.astype(F32) * egc - auw[:, n:]
            for c in ((0, 1) if d == 0 else (1, 0)):
                rs = slice(c * CHUNK, (c + 1) * CHUNK)
                end = c * CHUNK + (CHUNK - 1 if d == 0 else 0)
                g_last = gcol[end:end + 1]
                k_dec = kk[rs].astype(F32) * jnp.exp(g_last - gcol[rs])
                bm = lax.dot_general(k_dec.astype(BF16), uwb[rs], (((0,), (0,)), ((), ())),
                                     preferred_element_type=F32)
                s_old = state_ref[lane]
                lhs = jnp.concatenate([bm[:, n:], q_eff[rs]], axis=0).astype(BF16)
                ms = _dot(lhs, s_old.astype(BF16))
                state_ref[lane] = jnp.exp(g_last) * s_old - ms[:n] + bm[:, :n]
                o_ref[0, rs, h * DN_HEAD_DIM:(h + 1) * DN_HEAD_DIM] = ms[n:] + auw[rs, :n]


def _dn_scan(q, k, v, beta, gc, gct):
    b, _, l, _ = q.shape
    nb = l // SCAN_BLOCK
    fwd4 = lambda bi, i: (bi, 0, i, 0)
    bwd4 = lambda bi, i: (bi, 0, nb - 1 - i, 0)
    fwd3 = lambda bi, i: (bi, i, 0)
    bwd3 = lambda bi, i: (bi, nb - 1 - i, 0)
    fwdt = lambda bi, i: (bi, i, 0, 0)
    bwdt = lambda bi, i: (bi, nb - 1 - i, 0, 0)
    qkv_spec = lambda imap: pl.BlockSpec((1, DN_HEADS, SCAN_BLOCK, DN_HEAD_DIM), imap)
    gate_spec = lambda imap: pl.BlockSpec((1, SCAN_BLOCK, PROJ_PAD), imap)
    gct_spec = lambda imap: pl.BlockSpec((1, 1, 2 * DN_HEADS, SCAN_BLOCK), imap)
    out = jax.ShapeDtypeStruct((b, l, DN_WIDTH), F32)
    return pl.pallas_call(
        _dn_scan_kernel,
        out_shape=[out, out],
        grid=(b, nb),
        in_specs=[qkv_spec(fwd4), qkv_spec(fwd4), qkv_spec(fwd4),
                  qkv_spec(bwd4), qkv_spec(bwd4), qkv_spec(bwd4),
                  gate_spec(fwd3), gate_spec(fwd3), gct_spec(fwdt),
                  gate_spec(bwd3), gate_spec(bwd3), gct_spec(bwdt)],
        out_specs=[pl.BlockSpec((1, SCAN_BLOCK, DN_WIDTH), fwd3),
                   pl.BlockSpec((1, SCAN_BLOCK, DN_WIDTH), bwd3)],
        scratch_shapes=[pltpu.VMEM((2 * DN_HEADS, DN_HEAD_DIM, DN_HEAD_DIM), F32)],
        compiler_params=pltpu.CompilerParams(
            dimension_semantics=("arbitrary", "arbitrary"), vmem_limit_bytes=_vmem_limit(32 << 20)),
        name="dn_scan",
    )(q, k, v, q, k, v, beta, gc, gct, beta, gc, gct)


def _mix_out_kernel(alpha, oa_ref, of_ref, ob_ref, z_ref, x_ref, w_ref, ng_ref, g_ref, b_ref,
                    o_ref, dn_ref):
    for h in range(DN_HEADS):
        sl = slice(h * DN_HEAD_DIM, (h + 1) * DN_HEAD_DIM)
        o = of_ref[:, sl] + ob_ref[:, sl]
        ms = jnp.mean(o * o, axis=-1, keepdims=True)
        gate = _silu(z_ref[:, sl].astype(F32))
        dn_ref[:, sl] = (o * lax.rsqrt(ms + RMS_EPS) * ng_ref[...] * gate).astype(BF16)
    y = _dot(oa_ref[...], w_ref[:ATTN_WIDTH, :]) + _dot(dn_ref[...], w_ref[ATTN_WIDTH:, :])
    o_ref[...] = _layer_norm(alpha * x_ref[...] + y, g_ref[...], b_ref[...])


def _mix_out(oa, of, ob, z, x2, w_out, norm_gain, gain, bias, alpha):
    rows = x2.shape[0]
    row = lambda i: (i, 0)
    const = lambda i: (0, 0)
    return pl.pallas_call(
        functools.partial(_mix_out_kernel, alpha),
        out_shape=jax.ShapeDtypeStruct((rows, D_MODEL), F32),
        grid=(rows // ROW_TILE,),
        in_specs=[
            pl.BlockSpec((ROW_TILE, ATTN_WIDTH), row),
            pl.BlockSpec((ROW_TILE, DN_WIDTH), row),
            pl.BlockSpec((ROW_TILE, DN_WIDTH), row),
            pl.BlockSpec((ROW_TILE, DN_WIDTH), row),
            pl.BlockSpec((ROW_TILE, D_MODEL), row),
            pl.BlockSpec((D_MODEL, D_MODEL), const),
            pl.BlockSpec((1, DN_HEAD_DIM), const),
            pl.BlockSpec((1, D_MODEL), const),
            pl.BlockSpec((1, D_MODEL), const),
        ],
        out_specs=pl.BlockSpec((ROW_TILE, D_MODEL), row),
        scratch_shapes=[pltpu.VMEM((ROW_TILE, DN_WIDTH), BF16)],
        compiler_params=pltpu.CompilerParams(
            dimension_semantics=("arbitrary",), vmem_limit_bytes=_vmem_limit(40 << 20)),
        name="mix_out",
    )(oa, of, ob, z, x2, w_out, norm_gain, gain, bias)


def _pack_in_proj_weight(w_in):
    pad = lambda t: jnp.pad(t, ((0, 0), (0, PROJ_PAD - t.shape[1])))
    return jnp.concatenate([w_in[:, :OFF_B], pad(w_in[:, OFF_B:OFF_A]), pad(w_in[:, OFF_A:])],
                           axis=1).astype(BF16)


def _gate_params(a_log, dt_bias):
    rows = jnp.stack([a_log.reshape(-1), dt_bias.reshape(-1)]).astype(F32)
    return jnp.pad(rows, ((0, 8 - rows.shape[0]), (0, PROJ_PAD - rows.shape[1])))


def kernel(x_prompt, x_sample, ffn1_w_in, ffn1_w_out, w_in, conv_w, attn_sink, dn_a_log, dn_dt_bias,
           dn_norm_gain, w_out, ffn2_w_in, ffn2_w_out, ln_gain, ln_bias):
    depth = w_in.shape[0]
    alpha = (2.0 * depth) ** 0.25
    tri = _chunk_triangles()
    layers = []
    for i in range(depth):
        bias, sink = _attn_tables(attn_sink[i])
        layers.append(dict(
            ffn1_in=ffn1_w_in[i].astype(BF16), ffn1_out=ffn1_w_out[i].astype(BF16),
            ffn2_in=ffn2_w_in[i].astype(BF16), ffn2_out=ffn2_w_out[i].astype(BF16),
            w_all=_pack_in_proj_weight(w_in[i]), w_out=w_out[i].astype(BF16),
            conv=jnp.pad(conv_w[i].astype(F32), ((0, 8 - CONV_WIDTH), (0, 0))),
            gate=_gate_params(dn_a_log[i], dn_dt_bias[i]),
            norm_gain=dn_norm_gain[i].astype(F32).reshape(1, DN_HEAD_DIM),
            bias=bias, sink=sink,
            ln_g=ln_gain[i].astype(F32).reshape(3, 1, D_MODEL),
            ln_b=ln_bias[i].astype(F32).reshape(3, 1, D_MODEL),
        ))

    def trunk(x):
        b, l, _ = x.shape
        x2 = x.reshape(b * l, D_MODEL)
        for p in layers:
            x2 = _ffn_ln(x2, p["ffn1_in"], p["ffn1_out"], p["ln_g"][0], p["ln_b"][0], alpha)
            aq, ak, av, dqkv, z, bb, aa = _in_proj(x2, p["w_all"])
            oa = _attention(aq.reshape(b, l, ATTN_WIDTH), ak.reshape(b, l, KV_DIM),
                            av.reshape(b, l, KV_DIM), p["bias"], p["sink"])
            dq, dk, dv, beta, gc, gct = _dn_prep(
                dqkv.reshape(b, l, 3 * DN_WIDTH), bb.reshape(b, l, PROJ_PAD), aa.reshape(b, l, PROJ_PAD),
                p["conv"], p["gate"], tri)
            of, ob = _dn_scan(dq, dk, dv, beta, gc, gct)
            x2 = _mix_out(oa.reshape(b * l, ATTN_WIDTH), of.reshape(b * l, DN_WIDTH),
                          ob.reshape(b * l, DN_WIDTH), z, x2, p["w_out"], p["norm_gain"],
                          p["ln_g"][1], p["ln_b"][1], alpha)
            x2 = _ffn_ln(x2, p["ffn2_in"], p["ffn2_out"], p["ln_g"][2], p["ln_b"][2], alpha)
        return x2.reshape(b, l, D_MODEL)

    return (trunk(x_prompt), trunk(x_sample))
```

```python
import functools

import jax
import jax.numpy as jnp
from jax import lax
from jax.experimental import pallas as pl
from jax.experimental.pallas import tpu as pltpu

F32 = jnp.float32
BF16 = jnp.bfloat16

D_MODEL = 1024
ATTN_WIDTH = 512
ATTN_HEAD_DIM = 64
ATTN_HEADS = 8
ATTN_KV_HEADS = 2
ATTN_GROUP = ATTN_HEADS // ATTN_KV_HEADS
KV_DIM = ATTN_KV_HEADS * ATTN_HEAD_DIM
WINDOW = 128
BLOCK = 128
DN_WIDTH = 512
DN_HEAD_DIM = 128
DN_HEADS = DN_WIDTH // DN_HEAD_DIM
CONV_WIDTH = 5
CONV_PAD = CONV_WIDTH // 2
CHUNK = 64
D_FF = 2816
LN_EPS = 1e-5
RMS_EPS = 1e-6
OFF_B = ATTN_WIDTH + 2 * KV_DIM + 4 * DN_WIDTH
OFF_A = OFF_B + 2 * DN_HEADS

V7X_LANES = 128
V7X_BF16_SUBLANES = 16
V7X_VMEM_BYTES = 64 * 1024 * 1024

ROW_TILE = 512
FFN_CHUNK = 256
ATTN_QBLOCKS = 1
PREP_TILE = 256
SCAN_BLOCK = 2 * CHUNK
PROJ_PAD = V7X_LANES
NEG_BIG = -1e30


def _vmem_limit(nbytes):
    return int(min(nbytes, V7X_VMEM_BYTES - 4 * 1024 * 1024))


def _layer_norm(r, gain, bias):
    mu = jnp.mean(r, axis=-1, keepdims=True)
    c = r - mu
    var = jnp.mean(c * c, axis=-1, keepdims=True)
    return c * lax.rsqrt(var + LN_EPS) * gain + bias


def _silu(t):
    return t * jax.nn.sigmoid(t)


def _dot(a, b):
    return jnp.dot(a, b, preferred_element_type=F32)


def _ffn_ln_kernel(alpha, x_ref, win_ref, wout_ref, g_ref, b_ref, o_ref, act_ref):
    x = x_ref[...]
    xb = x.astype(BF16)
    for j in range(D_FF // FFN_CHUNK):
        lo = j * FFN_CHUNK
        gate = _dot(xb, win_ref[:, lo:lo + FFN_CHUNK])
        up = _dot(xb, win_ref[:, D_FF + lo:D_FF + lo + FFN_CHUNK])
        act_ref[:, lo:lo + FFN_CHUNK] = (_silu(gate) * up).astype(BF16)
    y = _dot(act_ref[...], wout_ref[...])
    o_ref[...] = _layer_norm(alpha * x + 0.5 * y, g_ref[...], b_ref[...])


def _ffn_ln(x2, w_in, w_out, gain, bias, alpha):
    rows = x2.shape[0]
    const = lambda i: (0, 0)
    return pl.pallas_call(
        functools.partial(_ffn_ln_kernel, alpha),
        out_shape=jax.ShapeDtypeStruct((rows, D_MODEL), F32),
        grid=(rows // ROW_TILE,),
        in_specs=[
            pl.BlockSpec((ROW_TILE, D_MODEL), lambda i: (i, 0)),
            pl.BlockSpec((D_MODEL, 2 * D_FF), const, pipeline_mode=pl.Buffered(1)),
            pl.BlockSpec((D_FF, D_MODEL), const, pipeline_mode=pl.Buffered(1)),
            pl.BlockSpec((1, D_MODEL), const),
            pl.BlockSpec((1, D_MODEL), const),
        ],
        out_specs=pl.BlockSpec((ROW_TILE, D_MODEL), lambda i: (i, 0)),
        scratch_shapes=[pltpu.VMEM((ROW_TILE, D_FF), BF16)],
        compiler_params=pltpu.CompilerParams(
            dimension_semantics=("arbitrary",), vmem_limit_bytes=_vmem_limit(48 << 20)),
        name="ffn_ln",
    )(x2, w_in, w_out, gain, bias)


_PROJ_SPLITS = (
    ("aq", 0, ATTN_WIDTH, BF16),
    ("ak", ATTN_WIDTH, KV_DIM, BF16),
    ("av", ATTN_WIDTH + KV_DIM, KV_DIM, BF16),
    ("dqkv", ATTN_WIDTH + 2 * KV_DIM, 3 * DN_WIDTH, BF16),
    ("z", ATTN_WIDTH + 2 * KV_DIM + 3 * DN_WIDTH, DN_WIDTH, BF16),
    ("bb", OFF_B, PROJ_PAD, F32),
    ("aa", OFF_B + PROJ_PAD, PROJ_PAD, F32),
)
_PROJ_COLS = OFF_B + 2 * PROJ_PAD


def _in_proj_kernel(x_ref, w_ref, *out_refs):
    xb = x_ref[...].astype(BF16)
    for (_, lo, width, dtype), o_ref in zip(_PROJ_SPLITS, out_refs):
        step = min(width, ATTN_WIDTH)
        for c in range(0, width, step):
            o_ref[:, c:c + step] = _dot(xb, w_ref[:, lo + c:lo + c + step]).astype(dtype)


def _in_proj(x2, w_all):
    rows = x2.shape[0]
    return pl.pallas_call(
        _in_proj_kernel,
        out_shape=[jax.ShapeDtypeStruct((rows, width), dtype) for _, _, width, dtype in _PROJ_SPLITS],
        grid=(rows // ROW_TILE,),
        in_specs=[
            pl.BlockSpec((ROW_TILE, D_MODEL), lambda i: (i, 0)),
            pl.BlockSpec((D_MODEL, _PROJ_COLS), lambda i: (0, 0), pipeline_mode=pl.Buffered(1)),
        ],
        out_specs=[pl.BlockSpec((ROW_TILE, width), lambda i: (i, 0)) for _, _, width, _ in _PROJ_SPLITS],
        compiler_params=pltpu.CompilerParams(
            dimension_semantics=("arbitrary",), vmem_limit_bytes=_vmem_limit(40 << 20)),
        name="in_proj",
    )(x2, w_all)


def _attn_kernel(q_ref, kp_ref, kc_ref, kn_ref, vp_ref, vc_ref, vn_ref,
                 bias_first_ref, bias_mid_ref, bias_last_ref, sink_ref, o_ref):
    q = q_ref[0] * jnp.asarray(ATTN_HEAD_DIM ** -0.5, BF16)
    kcat = jnp.concatenate([kp_ref[0], kc_ref[0], kn_ref[0]], axis=0).astype(F32)
    vcat = jnp.concatenate([vp_ref[0], vc_ref[0], vn_ref[0]], axis=0).astype(F32)
    krot = pltpu.roll(kcat, ATTN_HEAD_DIM, axis=1)
    vrot = pltpu.roll(vcat, ATTN_HEAD_DIM, axis=1)
    low_half = lax.broadcasted_iota(jnp.int32, kcat.shape, 1) < ATTN_HEAD_DIM
    group_of_lane = lax.broadcasted_iota(jnp.int32, (BLOCK, ATTN_GROUP * ATTN_HEAD_DIM), 1) // ATTN_HEAD_DIM
    k_reps, v_reps = [], []
    for h in range(ATTN_KV_HEADS):
        k_pair = jnp.where(low_half, kcat, krot) if h == 0 else jnp.where(low_half, krot, kcat)
        v_pair = jnp.where(low_half, vcat, vrot) if h == 0 else jnp.where(low_half, vrot, vcat)
        k_reps.append(jnp.concatenate([k_pair, k_pair], axis=1).astype(BF16))
        v_reps.append(jnp.concatenate([v_pair, v_pair], axis=1).astype(BF16))
    items = [(j, h) for j in range(ATTN_QBLOCKS) for h in range(ATTN_KV_HEADS)]
    scores, outs = {}, {}
    for j, h in items:
        qh = q[j * BLOCK:(j + 1) * BLOCK, h * 256:(h + 1) * 256]
        q_stack = jnp.concatenate(
            [jnp.where(group_of_lane == g, qh, jnp.zeros_like(qh)) for g in range(ATTN_GROUP)], axis=0)
        scores[j, h] = lax.dot_general(q_stack, k_reps[h][j * BLOCK:(j + 3) * BLOCK],
                                       (((1,), (1,)), ((), ())), preferred_element_type=F32)
    for j, h in items:
        bias_ref = bias_first_ref if j == 0 else (bias_last_ref if j == ATTN_QBLOCKS - 1 else bias_mid_ref)
        logits = scores[j, h] + bias_ref[0, h]
        sink = sink_ref[h][:, :1]
        m = jnp.maximum(jnp.max(logits, axis=-1, keepdims=True), sink)
        p = jnp.exp(logits - m)
        denom = jnp.sum(p, axis=-1, keepdims=True) + jnp.exp(sink - m)
        pv = _dot(p.astype(BF16), v_reps[h][j * BLOCK:(j + 3) * BLOCK]) / denom
        out = jnp.zeros((BLOCK, ATTN_GROUP * ATTN_HEAD_DIM), F32)
        for g in range(ATTN_GROUP):
            out = out + jnp.where(group_of_lane == g, pv[g * BLOCK:(g + 1) * BLOCK], 0.0)
        outs[j, h] = out
    for j in range(ATTN_QBLOCKS):
        o_ref[0, j * BLOCK:(j + 1) * BLOCK, :] = jnp.concatenate(
            [outs[j, h] for h in range(ATTN_KV_HEADS)], axis=1).astype(BF16)


def _attention(q, k, v, bias, sink):
    b, l, _ = q.shape
    qb = ATTN_QBLOCKS
    steps = l // (qb * BLOCK)
    nb = l // BLOCK
    cur = lambda bi, n: (bi, n, 0)
    prev = lambda bi, n: (bi, jnp.maximum(n * qb - 1, 0), 0)
    nxt = lambda bi, n: (bi, jnp.minimum((n + 1) * qb, nb - 1), 0)
    const = lambda bi, n: (0, 0, 0)
    no_prev = lambda n: (n == 0).astype(jnp.int32)
    no_next = lambda n: 2 * (n == steps - 1).astype(jnp.int32)
    first = lambda bi, n: (no_prev(n) + (no_next(n) if qb == 1 else 0), 0, 0, 0)
    mid = lambda bi, n: (0, 0, 0, 0)
    last = lambda bi, n: (no_next(n), 0, 0, 0)
    halo_spec = lambda imap: pl.BlockSpec((1, BLOCK, KV_DIM), imap)
    own_spec = pl.BlockSpec((1, qb * BLOCK, KV_DIM), cur)
    bias_spec = lambda imap: pl.BlockSpec((1, ATTN_KV_HEADS, ATTN_GROUP * BLOCK, 3 * BLOCK), imap)
    return pl.pallas_call(
        _attn_kernel,
        out_shape=jax.ShapeDtypeStruct((b, l, ATTN_WIDTH), BF16),
        grid=(b, steps),
        in_specs=[
            pl.BlockSpec((1, qb * BLOCK, ATTN_WIDTH), cur),
            halo_spec(prev), own_spec, halo_spec(nxt),
            halo_spec(prev), own_spec, halo_spec(nxt),
            bias_spec(first), bias_spec(mid), bias_spec(last),
            pl.BlockSpec((ATTN_KV_HEADS, ATTN_GROUP * BLOCK, V7X_LANES), const),
        ],
        out_specs=pl.BlockSpec((1, qb * BLOCK, ATTN_WIDTH), cur),
        compiler_params=pltpu.CompilerParams(
            dimension_semantics=("arbitrary", "arbitrary"), vmem_limit_bytes=_vmem_limit(40 << 20)),
        name="swa_attention",
    )(q, k, k, k, v, v, v, bias, bias, bias, sink)


def _attn_tables(attn_sink):
    slopes = 2.0 ** (-8.0 * jnp.arange(1, ATTN_HEADS + 1, dtype=F32) / ATTN_HEADS)
    qi = jnp.arange(BLOCK)[:, None]
    kj = jnp.arange(3 * BLOCK)[None, :]
    dist = jnp.abs(qi - kj + BLOCK)
    bias = jnp.where(dist <= WINDOW, -slopes[:, None, None] * dist.astype(F32)[None], NEG_BIG)
    bias = bias.reshape(ATTN_KV_HEADS, ATTN_GROUP * BLOCK, 3 * BLOCK)
    no_prev = kj < BLOCK
    no_next = kj >= 2 * BLOCK
    bias = jnp.stack([jnp.where(hidden, NEG_BIG, bias)
                      for hidden in (jnp.zeros_like(no_prev), no_prev, no_next, no_prev | no_next)])
    sink = jnp.broadcast_to(attn_sink.astype(F32)[:, None, None], (ATTN_HEADS, BLOCK, V7X_LANES))
    return bias, sink.reshape(ATTN_KV_HEADS, ATTN_GROUP * BLOCK, V7X_LANES)


def _dn_prep_kernel(x_ref, xp_ref, xn_ref, w_ref, bb_ref, aa_ref, gp_ref, tri_ref,
                    q_ref, k_ref, v_ref, beta_ref, gc_ref, gct_ref):
    i = pl.program_id(1)
    last = pl.num_programs(1) - 1
    halo = V7X_BF16_SUBLANES
    prev = jnp.where(i > 0, xp_ref[0].astype(F32), 0.0)
    nxt = jnp.where(i < last, xn_ref[0].astype(F32), 0.0)
    out_refs = (q_ref, k_ref, v_ref)
    for c in range(3 * DN_HEADS):
        kind, h = divmod(c, DN_HEADS)
        sl = slice(c * DN_HEAD_DIM, (c + 1) * DN_HEAD_DIM)
        xe = jnp.concatenate([prev[:, sl], x_ref[0, :, sl].astype(F32), nxt[:, sl]], axis=0)
        rows = xe.shape[0]
        y = jnp.zeros((PREP_TILE, DN_HEAD_DIM), F32)
        for t in range(CONV_WIDTH):
            shift = (CONV_PAD - t) % rows
            xs = pltpu.roll(xe, shift, axis=0) if shift else xe
            y = y + w_ref[t:t + 1, sl] * xs[halo:halo + PREP_TILE]
        s = _silu(y)
        if kind < 2:
            s = s * lax.rsqrt(jnp.sum(s * s, axis=-1, keepdims=True) + RMS_EPS)
        if kind == 0:
            s = s * (DN_HEAD_DIM ** -0.5)
        out_refs[kind][0, h] = s.astype(BF16)

    beta_ref[0] = jax.nn.sigmoid(bb_ref[0])
    t = aa_ref[0] + gp_ref[1:2, :]
    softplus = jnp.maximum(t, 0.0) + jnp.log1p(jnp.exp(-jnp.abs(t)))
    g = -jnp.exp(gp_ref[0:1, :]) * softplus
    prefix = jnp.dot(tri_ref[0], g, preferred_element_type=F32, precision=lax.Precision.HIGHEST)
    suffix = jnp.dot(tri_ref[1], g, preferred_element_type=F32, precision=lax.Precision.HIGHEST)
    lane = lax.broadcasted_iota(jnp.int32, g.shape, 1)
    gc = jnp.where(lane < DN_HEADS, prefix, suffix)
    gc_ref[0] = gc
    for r in range(PREP_TILE // SCAN_BLOCK):
        gct_ref[0, r] = gc[r * SCAN_BLOCK:(r + 1) * SCAN_BLOCK, :].T[0:2 * DN_HEADS, :]


def _dn_prep(dqkv, bb, aa, conv_w, gate_params, tri):
    b, l, _ = dqkv.shape
    halo = V7X_BF16_SUBLANES
    per = PREP_TILE // halo
    nh = l // halo
    cur = lambda bi, i: (bi, i, 0)
    const2 = lambda bi, i: (0, 0)
    head_major = jax.ShapeDtypeStruct((b, DN_HEADS, l, DN_HEAD_DIM), BF16)
    gate = jax.ShapeDtypeStruct((b, l, PROJ_PAD), F32)
    return pl.pallas_call(
        _dn_prep_kernel,
        out_shape=[head_major, head_major, head_major, gate, gate,
                   jax.ShapeDtypeStruct((b, l // SCAN_BLOCK, 2 * DN_HEADS, SCAN_BLOCK), F32)],
        grid=(b, l // PREP_TILE),
        in_specs=[
            pl.BlockSpec((1, PREP_TILE, 3 * DN_WIDTH), cur),
            pl.BlockSpec((1, halo, 3 * DN_WIDTH), lambda bi, i: (bi, jnp.maximum(i * per - 1, 0), 0)),
            pl.BlockSpec((1, halo, 3 * DN_WIDTH), lambda bi, i: (bi, jnp.minimum((i + 1) * per, nh - 1), 0)),
            pl.BlockSpec((8, 3 * DN_WIDTH), const2),
            pl.BlockSpec((1, PREP_TILE, PROJ_PAD), cur),
            pl.BlockSpec((1, PREP_TILE, PROJ_PAD), cur),
            pl.BlockSpec((8, PROJ_PAD), const2),
            pl.BlockSpec((2, PREP_TILE, PREP_TILE), lambda bi, i: (0, 0, 0)),
        ],
        out_specs=[
            pl.BlockSpec((1, DN_HEADS, PREP_TILE, DN_HEAD_DIM), lambda bi, i: (bi, 0, i, 0)),
            pl.BlockSpec((1, DN_HEADS, PREP_TILE, DN_HEAD_DIM), lambda bi, i: (bi, 0, i, 0)),
            pl.BlockSpec((1, DN_HEADS, PREP_TILE, DN_HEAD_DIM), lambda bi, i: (bi, 0, i, 0)),
            pl.BlockSpec((1, PREP_TILE, PROJ_PAD), cur),
            pl.BlockSpec((1, PREP_TILE, PROJ_PAD), cur),
            pl.BlockSpec((1, PREP_TILE // SCAN_BLOCK, 2 * DN_HEADS, SCAN_BLOCK), lambda bi, i: (bi, i, 0, 0)),
        ],
        compiler_params=pltpu.CompilerParams(
            dimension_semantics=("arbitrary", "arbitrary"), vmem_limit_bytes=_vmem_limit(32 << 20)),
        name="dn_prep",
    )(dqkv, dqkv, dqkv, conv_w, bb, aa, gate_params, tri)


def _chunk_triangles():
    t = jnp.arange(PREP_TILE)
    same = (t[:, None] // CHUNK) == (t[None, :] // CHUNK)
    prefix = same & (t[None, :] <= t[:, None])
    suffix = same & (t[None, :] >= t[:, None])
    return jnp.stack([prefix, suffix]).astype(F32)


def _dn_scan_kernel(qf_ref, kf_ref, vf_ref, qb_ref, kb_ref, vb_ref,
                    betaf_ref, gcf_ref, gctf_ref, betab_ref, gcb_ref, gctb_ref,
                    of_ref, ob_ref, state_ref):
    @pl.when(pl.program_id(1) == 0)
    def _():
        state_ref[...] = jnp.zeros_like(state_ref)

    n = SCAN_BLOCK
    row = lax.broadcasted_iota(jnp.int32, (n, n), 0)
    col = lax.broadcasted_iota(jnp.int32, (n, n), 1)
    same_chunk = (row // CHUNK) == (col // CHUNK)
    eye = (row == col).astype(F32)
    off_diag = row != col
    directions = (
        (qf_ref, kf_ref, vf_ref, betaf_ref, gcf_ref, gctf_ref, of_ref, same_chunk & (row >= col)),
        (qb_ref, kb_ref, vb_ref, betab_ref, gcb_ref, gctb_ref, ob_ref, same_chunk & (row <= col)),
    )
    probs = [(d, h) for d in range(2) for h in range(DN_HEADS)]
    kk, qq, vv, beta, gcol, egc, p, y, attn = ({} for _ in range(9))
    for pr in probs:
        d, h = pr
        q_ref, k_ref, v_ref, beta_ref, gc_ref, gct_ref, _, incl = directions[d]
        lane = d * DN_HEADS + h
        kk[pr] = k_ref[0, h]
        qq[pr] = q_ref[0, h]
        vv[pr] = v_ref[0, h]
        beta[pr] = beta_ref[0][:, lane:lane + 1]
        gcol[pr] = gc_ref[0][:, lane:lane + 1]
        grow = gct_ref[0, 0][lane:lane + 1, :]
        gram = lax.dot_general(jnp.concatenate([kk[pr], qq[pr]], axis=0), kk[pr],
                               (((1,), (1,)), ((), ())), preferred_element_type=F32)
        decay = jnp.exp(jnp.where(incl, gcol[pr] - grow, NEG_BIG))
        neg_n = -(beta[pr] * gram[:n] * jnp.where(off_diag, decay, 0.0))
        attn[pr] = (gram[n:] * decay).astype(BF16)
        p[pr] = eye + neg_n
        y[pr] = neg_n.astype(BF16)
    for pr in probs:
        y[pr] = _dot(y[pr], y[pr])
    for _ in range(4):
        for pr in probs:
            yb = y[pr].astype(BF16)
            r = _dot(yb, jnp.concatenate([p[pr].astype(BF16), yb], axis=1))
            p[pr] = p[pr] + r[:, :n]
            y[pr] = r[:, n:]
    for pr in probs:
        p[pr] = p[pr] + _dot(y[pr].astype(BF16), p[pr].astype(BF16))
    uwb, auw, q_eff = {}, {}, {}
    for pr in probs:
        egc[pr] = jnp.exp(gcol[pr])
        rhs = jnp.concatenate([vv[pr].astype(F32) * beta[pr],
                               kk[pr].astype(F32) * (beta[pr] * egc[pr])], axis=1)
        uwb[pr] = _dot(p[pr].astype(BF16), rhs.astype(BF16)).astype(BF16)
    for pr in probs:
        auw[pr] = _dot(attn[pr], uwb[pr])
        q_eff[pr] = qq[pr].astype(F32) * egc[pr] - auw[pr][:, n:]
    bm, g_last = {}, {}
    for step in range(2):
        for pr in probs:
            d, h = pr
            c = step if d == 0 else 1 - step
            rs = slice(c * CHUNK, (c + 1) * CHUNK)
            end = c * CHUNK + (CHUNK - 1 if d == 0 else 0)
            g_last[pr, step] = gcol[pr][end:end + 1]
            k_dec = kk[pr][rs].astype(F32) * jnp.exp(g_last[pr, step] - gcol[pr][rs])
            bm[pr, step] = lax.dot_general(k_dec.astype(BF16), uwb[pr][rs], (((0,), (0,)), ((), ())),
                                           preferred_element_type=F32)
    for step in range(2):
        for pr in probs:
            d, h = pr
            c = step if d == 0 else 1 - step
            rs = slice(c * CHUNK, (c + 1) * CHUNK)
            lane = d * DN_HEADS + h
            s_old = state_ref[lane]
            lhs = jnp.concatenate([bm[pr, step][:, n:], q_eff[pr][rs]], axis=0).astype(BF16)
            ms = _dot(lhs, s_old.astype(BF16))
            state_ref[lane] = jnp.exp(g_last[pr, step]) * s_old - ms[:n] + bm[pr, step][:, :n]
            directions[d][6][0, rs, h * DN_HEAD_DIM:(h + 1) * DN_HEAD_DIM] = (
                ms[n:] + auw[pr][rs, :n]).astype(BF16)


def _dn_scan(q, k, v, beta, gc, gct):
    b, _, l, _ = q.shape
    nb = l // SCAN_BLOCK
    fwd4 = lambda bi, i: (bi, 0, i, 0)
    bwd4 = lambda bi, i: (bi, 0, nb - 1 - i, 0)
    fwd3 = lambda bi, i: (bi, i, 0)
    bwd3 = lambda bi, i: (bi, nb - 1 - i, 0)
    fwdt = lambda bi, i: (bi, i, 0, 0)
    bwdt = lambda bi, i: (bi, nb - 1 - i, 0, 0)
    qkv_spec = lambda imap: pl.BlockSpec((1, DN_HEADS, SCAN_BLOCK, DN_HEAD_DIM), imap)
    gate_spec = lambda imap: pl.BlockSpec((1, SCAN_BLOCK, PROJ_PAD), imap)
    gct_spec = lambda imap: pl.BlockSpec((1, 1, 2 * DN_HEADS, SCAN_BLOCK), imap)
    out = jax.ShapeDtypeStruct((b, l, DN_WIDTH), BF16)
    return pl.pallas_call(
        _dn_scan_kernel,
        out_shape=[out, out],
        grid=(b, nb),
        in_specs=[qkv_spec(fwd4), qkv_spec(fwd4), qkv_spec(fwd4),
                  qkv_spec(bwd4), qkv_spec(bwd4), qkv_spec(bwd4),
                  gate_spec(fwd3), gate_spec(fwd3), gct_spec(fwdt),
                  gate_spec(bwd3), gate_spec(bwd3), gct_spec(bwdt)],
        out_specs=[pl.BlockSpec((1, SCAN_BLOCK, DN_WIDTH), fwd3),
                   pl.BlockSpec((1, SCAN_BLOCK, DN_WIDTH), bwd3)],
        scratch_shapes=[pltpu.VMEM((2 * DN_HEADS, DN_HEAD_DIM, DN_HEAD_DIM), F32)],
        compiler_params=pltpu.CompilerParams(
            dimension_semantics=("arbitrary", "arbitrary"), vmem_limit_bytes=_vmem_limit(32 << 20)),
        name="dn_scan",
    )(q, k, v, q, k, v, beta, gc, gct, beta, gc, gct)


def _mix_out_kernel(alpha, oa_ref, of_ref, ob_ref, z_ref, x_ref, w_ref, ng_ref, g_ref, b_ref,
                    o_ref, dn_ref):
    for h in range(DN_HEADS):
        sl = slice(h * DN_HEAD_DIM, (h + 1) * DN_HEAD_DIM)
        o = of_ref[:, sl].astype(F32) + ob_ref[:, sl].astype(F32)
        ms = jnp.mean(o * o, axis=-1, keepdims=True)
        gate = _silu(z_ref[:, sl].astype(F32))
        dn_ref[:, sl] = (o * lax.rsqrt(ms + RMS_EPS) * ng_ref[...] * gate).astype(BF16)
    y = _dot(oa_ref[...], w_ref[:ATTN_WIDTH, :]) + _dot(dn_ref[...], w_ref[ATTN_WIDTH:, :])
    o_ref[...] = _layer_norm(alpha * x_ref[...] + y, g_ref[...], b_ref[...])


def _mix_out(oa, of, ob, z, x2, w_out, norm_gain, gain, bias, alpha):
    rows = x2.shape[0]
    row = lambda i: (i, 0)
    const = lambda i: (0, 0)
    return pl.pallas_call(
        functools.partial(_mix_out_kernel, alpha),
        out_shape=jax.ShapeDtypeStruct((rows, D_MODEL), F32),
        grid=(rows // ROW_TILE,),
        in_specs=[
            pl.BlockSpec((ROW_TILE, ATTN_WIDTH), row),
            pl.BlockSpec((ROW_TILE, DN_WIDTH), row),
            pl.BlockSpec((ROW_TILE, DN_WIDTH), row),
            pl.BlockSpec((ROW_TILE, DN_WIDTH), row),
            pl.BlockSpec((ROW_TILE, D_MODEL), row),
            pl.BlockSpec((D_MODEL, D_MODEL), const),
            pl.BlockSpec((1, DN_HEAD_DIM), const),
            pl.BlockSpec((1, D_MODEL), const),
            pl.BlockSpec((1, D_MODEL), const),
        ],
        out_specs=pl.BlockSpec((ROW_TILE, D_MODEL), row),
        scratch_shapes=[pltpu.VMEM((ROW_TILE, DN_WIDTH), BF16)],
        compiler_params=pltpu.CompilerParams(
            dimension_semantics=("arbitrary",), vmem_limit_bytes=_vmem_limit(40 << 20)),
        name="mix_out",
    )(oa, of, ob, z, x2, w_out, norm_gain, gain, bias)


def _pack_in_proj_weight(w_in):
    pad = lambda t: jnp.pad(t, ((0, 0), (0, PROJ_PAD - t.shape[1])))
    return jnp.concatenate([w_in[:, :OFF_B], pad(w_in[:, OFF_B:OFF_A]), pad(w_in[:, OFF_A:])],
                           axis=1).astype(BF16)


def _gate_params(a_log, dt_bias):
    rows = jnp.stack([a_log.reshape(-1), dt_bias.reshape(-1)]).astype(F32)
    return jnp.pad(rows, ((0, 8 - rows.shape[0]), (0, PROJ_PAD - rows.shape[1])))


def kernel(x_prompt, x_sample, ffn1_w_in, ffn1_w_out, w_in, conv_w, attn_sink, dn_a_log, dn_dt_bias,
           dn_norm_gain, w_out, ffn2_w_in, ffn2_w_out, ln_gain, ln_bias):
    depth = w_in.shape[0]
    alpha = (2.0 * depth) ** 0.25
    tri = _chunk_triangles()
    layers = []
    for i in range(depth):
        bias, sink = _attn_tables(attn_sink[i])
        layers.append(dict(
            ffn1_in=ffn1_w_in[i].astype(BF16), ffn1_out=ffn1_w_out[i].astype(BF16),
            ffn2_in=ffn2_w_in[i].astype(BF16), ffn2_out=ffn2_w_out[i].astype(BF16),
            w_all=_pack_in_proj_weight(w_in[i]), w_out=w_out[i].astype(BF16),
            conv=jnp.pad(conv_w[i].astype(F32), ((0, 8 - CONV_WIDTH), (0, 0))),
            gate=_gate_params(dn_a_log[i], dn_dt_bias[i]),
            norm_gain=dn_norm_gain[i].astype(F32).reshape(1, DN_HEAD_DIM),
            bias=bias, sink=sink,
            ln_g=ln_gain[i].astype(F32).reshape(3, 1, D_MODEL),
            ln_b=ln_bias[i].astype(F32).reshape(3, 1, D_MODEL),
        ))

    def trunk(x):
        b, l, _ = x.shape
        x2 = x.reshape(b * l, D_MODEL)
        for p in layers:
            x2 = _ffn_ln(x2, p["ffn1_in"], p["ffn1_out"], p["ln_g"][0], p["ln_b"][0], alpha)
            aq, ak, av, dqkv, z, bb, aa = _in_proj(x2, p["w_all"])
            oa = _attention(aq.reshape(b, l, ATTN_WIDTH), ak.reshape(b, l, KV_DIM),
                            av.reshape(b, l, KV_DIM), p["bias"], p["sink"])
            dq, dk, dv, beta, gc, gct = _dn_prep(
                dqkv.reshape(b, l, 3 * DN_WIDTH), bb.reshape(b, l, PROJ_PAD), aa.reshape(b, l, PROJ_PAD),
                p["conv"], p["gate"], tri)
            of, ob = _dn_scan(dq, dk, dv, beta, gc, gct)
            x2 = _mix_out(oa.reshape(b * l, ATTN_WIDTH), of.reshape(b * l, DN_WIDTH),
                          ob.reshape(b * l, DN_WIDTH), z, x2, p["w_out"], p["norm_gain"],
                          p["ln_g"][1], p["ln_b"][1], alpha)
            x2 = _ffn_ln(x2, p["ffn2_in"], p["ffn2_out"], p["ln_g"][2], p["ln_b"][2], alpha)
        return x2.reshape(b, l, D_MODEL)

    return (trunk(x_prompt), trunk(x_sample))
```

```python
import functools

import jax
import jax.numpy as jnp
from jax import lax
from jax.experimental import pallas as pl
from jax.experimental.pallas import tpu as pltpu

F32 = jnp.float32
BF16 = jnp.bfloat16

D_MODEL = 1024
ATTN_WIDTH = 512
ATTN_HEAD_DIM = 64
ATTN_HEADS = 8
ATTN_KV_HEADS = 2
ATTN_GROUP = ATTN_HEADS // ATTN_KV_HEADS
KV_DIM = ATTN_KV_HEADS * ATTN_HEAD_DIM
WINDOW = 128
BLOCK = 128
DN_WIDTH = 512
DN_HEAD_DIM = 128
DN_HEADS = DN_WIDTH // DN_HEAD_DIM
CONV_WIDTH = 5
CONV_PAD = CONV_WIDTH // 2
CHUNK = 64
D_FF = 2816
LN_EPS = 1e-5
RMS_EPS = 1e-6
OFF_B = ATTN_WIDTH + 2 * KV_DIM + 4 * DN_WIDTH
OFF_A = OFF_B + 2 * DN_HEADS

V7X_LANES = 128
V7X_BF16_SUBLANES = 16
V7X_VMEM_BYTES = 64 * 1024 * 1024

ROW_TILE = 512
FFN_CHUNK = 256
ATTN_QBLOCKS = 4
MIX_TILE = 512
MIX_HALO = V7X_BF16_SUBLANES
SCAN_BLOCK = 2 * CHUNK
GATE_ROWS = 4 * DN_HEADS
NEG_BIG = -1e30


def _vmem_limit(nbytes):
    return int(min(nbytes, V7X_VMEM_BYTES - 4 * 1024 * 1024))


def _layer_norm(r, gain, bias):
    mu = jnp.mean(r, axis=-1, keepdims=True)
    c = r - mu
    var = jnp.mean(c * c, axis=-1, keepdims=True)
    return c * lax.rsqrt(var + LN_EPS) * gain + bias


def _silu(t):
    return t * jax.nn.sigmoid(t)


def _dot(a, b):
    return jnp.dot(a, b, preferred_element_type=F32)


def _ffn_ln_kernel(alpha, x_ref, win_ref, wout_ref, g_ref, b_ref, o_ref, act_ref):
    x = x_ref[...]
    xb = x.astype(BF16)
    for j in range(D_FF // FFN_CHUNK):
        lo = j * FFN_CHUNK
        gate = _dot(xb, win_ref[:, lo:lo + FFN_CHUNK])
        up = _dot(xb, win_ref[:, D_FF + lo:D_FF + lo + FFN_CHUNK])
        act_ref[:, lo:lo + FFN_CHUNK] = (_silu(gate) * up).astype(BF16)
    y = _dot(act_ref[...], wout_ref[...])
    o_ref[...] = _layer_norm(alpha * x + 0.5 * y, g_ref[...], b_ref[...])


def _ffn_ln(x2, w_in, w_out, gain, bias, alpha):
    rows = x2.shape[0]
    const = lambda i: (0, 0)
    return pl.pallas_call(
        functools.partial(_ffn_ln_kernel, alpha),
        out_shape=jax.ShapeDtypeStruct((rows, D_MODEL), F32),
        grid=(rows // ROW_TILE,),
        in_specs=[
            pl.BlockSpec((ROW_TILE, D_MODEL), lambda i: (i, 0)),
            pl.BlockSpec((D_MODEL, 2 * D_FF), const, pipeline_mode=pl.Buffered(1)),
            pl.BlockSpec((D_FF, D_MODEL), const, pipeline_mode=pl.Buffered(1)),
            pl.BlockSpec((1, D_MODEL), const),
            pl.BlockSpec((1, D_MODEL), const),
        ],
        out_specs=pl.BlockSpec((ROW_TILE, D_MODEL), lambda i: (i, 0)),
        scratch_shapes=[pltpu.VMEM((ROW_TILE, D_FF), BF16)],
        compiler_params=pltpu.CompilerParams(
            dimension_semantics=("arbitrary",), vmem_limit_bytes=_vmem_limit(48 << 20)),
        name="ffn_ln",
    )(x2, w_in, w_out, gain, bias)


_W_ROW_K = 0
_W_ROW_Z = KV_DIM
_W_ROW_DQKV = KV_DIM + DN_WIDTH
_W_ROW_COLS = KV_DIM + DN_WIDTH + 3 * DN_WIDTH
_W_T_Q = 0
_W_T_V = ATTN_WIDTH
_W_T_GATE = ATTN_WIDTH + KV_DIM
_W_T_ROWS = _W_T_GATE + GATE_ROWS


def _mixer_in_kernel(x_ref, xp_ref, xn_ref, wrow_ref, wt_ref, conv_ref, gp_ref, tri_ref,
                     qt_ref, k_ref, vt_ref, z_ref, dq_ref, dk_ref, dv_ref, gcol_ref, gct_ref):
    i = pl.program_id(1)
    last = pl.num_programs(1) - 1
    tile, halo = MIX_TILE, MIX_HALO
    prev = jnp.where(i > 0, xp_ref[0], 0.0)
    nxt = jnp.where(i < last, xn_ref[0], 0.0)
    xe = jnp.concatenate([prev, x_ref[0], nxt], axis=0).astype(BF16)
    xb = xe[halo:halo + tile]

    tproj = lax.dot_general(wt_ref[...], xb, (((1,), (1,)), ((), ())), preferred_element_type=F32)
    qt_ref[0] = tproj[_W_T_Q:_W_T_V].astype(BF16)
    vt_ref[0] = tproj[_W_T_V:_W_T_GATE].astype(BF16)
    gate_logits = tproj[_W_T_GATE:_W_T_ROWS]

    k_ref[0] = _dot(xb, wrow_ref[:, _W_ROW_K:_W_ROW_Z]).astype(BF16)
    z_ref[0] = _dot(xb, wrow_ref[:, _W_ROW_Z:_W_ROW_DQKV]).astype(BF16)

    rows = tile + 2 * halo
    for kind, o_ref in enumerate((dq_ref, dk_ref, dv_ref)):
        lo = _W_ROW_DQKV + kind * DN_WIDTH
        proj = _dot(xe, wrow_ref[:, lo:lo + DN_WIDTH])
        for h in range(DN_HEADS):
            col = kind * DN_WIDTH + h * DN_HEAD_DIM
            xcol = proj[:, h * DN_HEAD_DIM:(h + 1) * DN_HEAD_DIM]
            y = jnp.zeros((tile, DN_HEAD_DIM), F32)
            for t in range(CONV_WIDTH):
                shift = (CONV_PAD - t) % rows
                xs = pltpu.roll(xcol, shift, axis=0) if shift else xcol
                y = y + conv_ref[t:t + 1, col:col + DN_HEAD_DIM] * xs[halo:halo + tile]
            s = _silu(y)
            if kind < 2:
                s = s * lax.rsqrt(jnp.sum(s * s, axis=-1, keepdims=True) + RMS_EPS)
            if kind == 0:
                s = s * (DN_HEAD_DIM ** -0.5)
            o_ref[0, h] = s.astype(BF16)

    row = lax.broadcasted_iota(jnp.int32, (GATE_ROWS, SCAN_BLOCK), 0)
    beta = jax.nn.sigmoid(gate_logits)
    t = gate_logits + gp_ref[:, 1:2]
    softplus = jnp.maximum(t, 0.0) + jnp.log1p(jnp.exp(-jnp.abs(t)))
    g = -jnp.exp(gp_ref[:, 0:1]) * softplus
    g_hi = g.astype(BF16)
    r1 = g - g_hi.astype(F32)
    g_mid = r1.astype(BF16)
    g_lo = (r1 - g_mid.astype(F32)).astype(BF16)
    is_fwd = row < 2 * DN_HEADS + DN_HEADS
    pad_rows = jnp.zeros((SCAN_BLOCK - GATE_ROWS, SCAN_BLOCK), F32)
    for r in range(tile // SCAN_BLOCK):
        ls = slice(r * SCAN_BLOCK, (r + 1) * SCAN_BLOCK)
        parts = _dot(jnp.concatenate([g_hi[:, ls], g_mid[:, ls], g_lo[:, ls]], axis=0), tri_ref[...])
        sums = parts[:GATE_ROWS] + parts[GATE_ROWS:2 * GATE_ROWS] + parts[2 * GATE_ROWS:]
        gc = jnp.where(is_fwd, sums[:, :SCAN_BLOCK], sums[:, SCAN_BLOCK:])
        gct_ref[0, r] = gc[2 * DN_HEADS:]
        both = jnp.where(row < 2 * DN_HEADS, beta[:, ls], gc)
        gcol_ref[0, ls, :] = jnp.concatenate([both, pad_rows], axis=0).T


def _mixer_in(x, w_row, w_t, conv_w, gate_params, tri):
    b, l, _ = x.shape
    per = MIX_TILE // MIX_HALO
    nh = l // MIX_HALO
    cur = lambda bi, i: (bi, i, 0)
    const2 = lambda bi, i: (0, 0)
    head_major = jax.ShapeDtypeStruct((b, DN_HEADS, l, DN_HEAD_DIM), BF16)
    head_spec = pl.BlockSpec((1, DN_HEADS, MIX_TILE, DN_HEAD_DIM), lambda bi, i: (bi, 0, i, 0))
    return pl.pallas_call(
        _mixer_in_kernel,
        out_shape=[
            jax.ShapeDtypeStruct((b, ATTN_WIDTH, l), BF16),
            jax.ShapeDtypeStruct((b, l, KV_DIM), BF16),
            jax.ShapeDtypeStruct((b, KV_DIM, l), BF16),
            jax.ShapeDtypeStruct((b, l, DN_WIDTH), BF16),
            head_major, head_major, head_major,
            jax.ShapeDtypeStruct((b, l, V7X_LANES), F32),
            jax.ShapeDtypeStruct((b, l // SCAN_BLOCK, 2 * DN_HEADS, SCAN_BLOCK), F32),
        ],
        grid=(b, l // MIX_TILE),
        in_specs=[
            pl.BlockSpec((1, MIX_TILE, D_MODEL), cur),
            pl.BlockSpec((1, MIX_HALO, D_MODEL), lambda bi, i: (bi, jnp.maximum(i * per - 1, 0), 0)),
            pl.BlockSpec((1, MIX_HALO, D_MODEL), lambda bi, i: (bi, jnp.minimum((i + 1) * per, nh - 1), 0)),
            pl.BlockSpec((D_MODEL, _W_ROW_COLS), const2, pipeline_mode=pl.Buffered(1)),
            pl.BlockSpec((_W_T_ROWS, D_MODEL), const2, pipeline_mode=pl.Buffered(1)),
            pl.BlockSpec((8, 3 * DN_WIDTH), const2),
            pl.BlockSpec((GATE_ROWS, V7X_LANES), const2),
            pl.BlockSpec((SCAN_BLOCK, 2 * SCAN_BLOCK), const2),
        ],
        out_specs=[
            pl.BlockSpec((1, ATTN_WIDTH, MIX_TILE), lambda bi, i: (bi, 0, i)),
            pl.BlockSpec((1, MIX_TILE, KV_DIM), cur),
            pl.BlockSpec((1, KV_DIM, MIX_TILE), lambda bi, i: (bi, 0, i)),
            pl.BlockSpec((1, MIX_TILE, DN_WIDTH), cur),
            head_spec, head_spec, head_spec,
            pl.BlockSpec((1, MIX_TILE, V7X_LANES), cur),
            pl.BlockSpec((1, MIX_TILE // SCAN_BLOCK, 2 * DN_HEADS, SCAN_BLOCK), lambda bi, i: (bi, i, 0, 0)),
        ],
        compiler_params=pltpu.CompilerParams(
            dimension_semantics=("arbitrary", "arbitrary"), vmem_limit_bytes=_vmem_limit(48 << 20)),
        name="mixer_in",
    )(x, x, x, w_row, w_t, conv_w, gate_params, tri)


def _chunk_triangles():
    t = jnp.arange(SCAN_BLOCK)
    same = (t[:, None] // CHUNK) == (t[None, :] // CHUNK)
    prefix = same & (t[:, None] <= t[None, :])
    suffix = same & (t[:, None] >= t[None, :])
    return jnp.concatenate([prefix, suffix], axis=1).astype(BF16)


def _attn_kernel(qt_ref, kp_ref, kc_ref, kn_ref, vtp_ref, vtc_ref, vtn_ref,
                 bias_first_ref, bias_mid_ref, bias_last_ref, sink_ref, o_ref):
    qt = qt_ref[0] * jnp.asarray(ATTN_HEAD_DIM ** -0.5, BF16)
    kcat = jnp.concatenate([kp_ref[0], kc_ref[0], kn_ref[0]], axis=0)
    vtcat = jnp.concatenate([vtp_ref[0], vtc_ref[0], vtn_ref[0]], axis=1)
    no_head = jnp.zeros((ATTN_HEAD_DIM, BLOCK), BF16)
    items = [(j, h) for j in range(ATTN_QBLOCKS) for h in range(ATTN_KV_HEADS)]
    scores, outs = {}, {}
    for j, h in items:
        cols = []
        for g in range(ATTN_GROUP):
            lo = (h * ATTN_GROUP + g) * ATTN_HEAD_DIM
            qhg = qt[lo:lo + ATTN_HEAD_DIM, j * BLOCK:(j + 1) * BLOCK]
            cols.append(jnp.concatenate([qhg, no_head] if h == 0 else [no_head, qhg], axis=0))
        scores[j, h] = _dot(kcat[j * BLOCK:(j + 3) * BLOCK], jnp.concatenate(cols, axis=1))
    for j, h in items:
        bias_ref = bias_first_ref if j == 0 else (bias_last_ref if j == ATTN_QBLOCKS - 1 else bias_mid_ref)
        logits = scores[j, h] + bias_ref[0, h]
        sink = sink_ref[h][0:1, :]
        m = jnp.maximum(jnp.max(logits, axis=0, keepdims=True), sink)
        p = jnp.exp(logits - m)
        denom = jnp.sum(p, axis=0, keepdims=True) + jnp.exp(sink - m)
        vt_h = vtcat[h * ATTN_HEAD_DIM:(h + 1) * ATTN_HEAD_DIM, j * BLOCK:(j + 3) * BLOCK]
        pv = _dot(vt_h, p.astype(BF16)) / denom
        outs[j, h] = [pv[:, g * BLOCK:(g + 1) * BLOCK] for g in range(ATTN_GROUP)]
    for j in range(ATTN_QBLOCKS):
        pieces = [piece for h in range(ATTN_KV_HEADS) for piece in outs[j, h]]
        o_ref[0, j * BLOCK:(j + 1) * BLOCK, :] = jnp.concatenate(pieces, axis=0).T.astype(BF16)


def _attention(qt, k, vt, bias, sink):
    b, _, l = qt.shape
    qb = ATTN_QBLOCKS
    nb = l // BLOCK
    steps = nb // qb
    before = lambda n: jnp.maximum(n * qb - 1, 0)
    after = lambda n: jnp.minimum((n + 1) * qb, nb - 1)
    no_prev = lambda n: (n == 0).astype(jnp.int32)
    no_next = lambda n: 2 * (n == steps - 1).astype(jnp.int32)
    first = lambda bi, n: (no_prev(n) + (no_next(n) if qb == 1 else 0), 0, 0, 0)
    mid = lambda bi, n: (0, 0, 0, 0)
    last = lambda bi, n: (no_next(n), 0, 0, 0)
    k_halo = lambda f: pl.BlockSpec((1, BLOCK, KV_DIM), lambda bi, n: (bi, f(n), 0))
    vt_halo = lambda f: pl.BlockSpec((1, KV_DIM, BLOCK), lambda bi, n: (bi, 0, f(n)))
    bias_spec = lambda imap: pl.BlockSpec((1, ATTN_KV_HEADS, 3 * BLOCK, ATTN_GROUP * BLOCK), imap)
    return pl.pallas_call(
        _attn_kernel,
        out_shape=jax.ShapeDtypeStruct((b, l, ATTN_WIDTH), BF16),
        grid=(b, steps),
        in_specs=[
            pl.BlockSpec((1, ATTN_WIDTH, qb * BLOCK), lambda bi, n: (bi, 0, n)),
            k_halo(before), pl.BlockSpec((1, qb * BLOCK, KV_DIM), lambda bi, n: (bi, n, 0)), k_halo(after),
            vt_halo(before), pl.BlockSpec((1, KV_DIM, qb * BLOCK), lambda bi, n: (bi, 0, n)), vt_halo(after),
            bias_spec(first), bias_spec(mid), bias_spec(last),
            pl.BlockSpec((ATTN_KV_HEADS, 8, ATTN_GROUP * BLOCK), lambda bi, n: (0, 0, 0)),
        ],
        out_specs=pl.BlockSpec((1, qb * BLOCK, ATTN_WIDTH), lambda bi, n: (bi, n, 0)),
        compiler_params=pltpu.CompilerParams(
            dimension_semantics=("arbitrary", "arbitrary"), vmem_limit_bytes=_vmem_limit(40 << 20)),
        name="swa_attention",
    )(qt, k, k, k, vt, vt, vt, bias, bias, bias, sink)


def _attn_tables(attn_sink):
    slopes = 2.0 ** (-8.0 * jnp.arange(1, ATTN_HEADS + 1, dtype=F32) / ATTN_HEADS)
    kj = jnp.arange(3 * BLOCK)[:, None]
    qi = jnp.arange(BLOCK)[None, :]
    dist = jnp.abs(qi - kj + BLOCK)
    bias = jnp.where(dist <= WINDOW, -slopes[:, None, None] * dist.astype(F32)[None], NEG_BIG)
    bias = bias.reshape(ATTN_KV_HEADS, ATTN_GROUP, 3 * BLOCK, BLOCK).transpose(0, 2, 1, 3)
    bias = bias.reshape(ATTN_KV_HEADS, 3 * BLOCK, ATTN_GROUP * BLOCK)
    no_prev = (kj < BLOCK)[None]
    no_next = (kj >= 2 * BLOCK)[None]
    bias = jnp.stack([jnp.where(hidden, NEG_BIG, bias)
                      for hidden in (jnp.zeros_like(no_prev), no_prev, no_next, no_prev | no_next)])
    sink = jnp.broadcast_to(attn_sink.astype(F32).reshape(ATTN_KV_HEADS, 1, ATTN_GROUP, 1),
                            (ATTN_KV_HEADS, 8, ATTN_GROUP, BLOCK))
    return bias, sink.reshape(ATTN_KV_HEADS, 8, ATTN_GROUP * BLOCK)


def _dn_scan_kernel(qf_ref, kf_ref, vf_ref, qb_ref, kb_ref, vb_ref,
                    gcolf_ref, gctf_ref, gcolb_ref, gctb_ref,
                    of_ref, ob_ref, state_ref):
    @pl.when(pl.program_id(1) == 0)
    def _():
        state_ref[...] = jnp.zeros_like(state_ref)

    n = SCAN_BLOCK
    row = lax.broadcasted_iota(jnp.int32, (n, n), 0)
    col = lax.broadcasted_iota(jnp.int32, (n, n), 1)
    same_chunk = (row // CHUNK) == (col // CHUNK)
    eye = (row == col).astype(F32)
    off_diag = row != col
    directions = (
        (qf_ref, kf_ref, vf_ref, gcolf_ref, gctf_ref, of_ref, same_chunk & (row >= col)),
        (qb_ref, kb_ref, vb_ref, gcolb_ref, gctb_ref, ob_ref, same_chunk & (row <= col)),
    )
    probs = [(d, h) for d in range(2) for h in range(DN_HEADS)]
    kk, qq, vv, beta, gcol, egc, p, y, attn = ({} for _ in range(9))
    for pr in probs:
        d, h = pr
        q_ref, k_ref, v_ref, gcol_ref, gct_ref, _, incl = directions[d]
        lane = d * DN_HEADS + h
        kk[pr] = k_ref[0, h]
        qq[pr] = q_ref[0, h]
        vv[pr] = v_ref[0, h]
        beta[pr] = gcol_ref[0][:, lane:lane + 1]
        gcol[pr] = gcol_ref[0][:, 2 * DN_HEADS + lane:2 * DN_HEADS + lane + 1]
        grow = gct_ref[0, 0][lane:lane + 1, :]
        gram = lax.dot_general(jnp.concatenate([kk[pr], qq[pr]], axis=0), kk[pr],
                               (((1,), (1,)), ((), ())), preferred_element_type=F32)
        decay = jnp.exp(jnp.where(incl, gcol[pr] - grow, NEG_BIG))
        neg_n = -(beta[pr] * gram[:n] * jnp.where(off_diag, decay, 0.0))
        attn[pr] = (gram[n:] * decay).astype(BF16)
        p[pr] = eye + neg_n
        y[pr] = neg_n.astype(BF16)
    for pr in probs:
        y[pr] = _dot(y[pr], y[pr])
    for _ in range(4):
        for pr in probs:
            yb = y[pr].astype(BF16)
            r = _dot(yb, jnp.concatenate([p[pr].astype(BF16), yb], axis=1))
            p[pr] = p[pr] + r[:, :n]
            y[pr] = r[:, n:]
    for pr in probs:
        p[pr] = p[pr] + _dot(y[pr].astype(BF16), p[pr].astype(BF16))
    uwb, auw, q_eff = {}, {}, {}
    for pr in probs:
        egc[pr] = jnp.exp(gcol[pr])
        rhs = jnp.concatenate([vv[pr].astype(F32) * beta[pr],
                               kk[pr].astype(F32) * (beta[pr] * egc[pr])], axis=1)
        uwb[pr] = _dot(p[pr].astype(BF16), rhs.astype(BF16)).astype(BF16)
    for pr in probs:
        auw[pr] = _dot(attn[pr], uwb[pr])
        q_eff[pr] = qq[pr].astype(F32) * egc[pr] - auw[pr][:, n:]
    bm, g_last = {}, {}
    for step in range(2):
        for pr in probs:
            d, h = pr
            c = step if d == 0 else 1 - step
            rs = slice(c * CHUNK, (c + 1) * CHUNK)
            end = c * CHUNK + (CHUNK - 1 if d == 0 else 0)
            g_last[pr, step] = gcol[pr][end:end + 1]
            k_dec = kk[pr][rs].astype(F32) * jnp.exp(g_last[pr, step] - gcol[pr][rs])
            bm[pr, step] = lax.dot_general(k_dec.astype(BF16), uwb[pr][rs], (((0,), (0,)), ((), ())),
                                           preferred_element_type=F32)
    for step in range(2):
        for pr in probs:
            d, h = pr
            c = step if d == 0 else 1 - step
            rs = slice(c * CHUNK, (c + 1) * CHUNK)
            lane = d * DN_HEADS + h
            s_old = state_ref[lane]
            lhs = jnp.concatenate([bm[pr, step][:, n:], q_eff[pr][rs]], axis=0).astype(BF16)
            ms = _dot(lhs, s_old.astype(BF16))
            state_ref[lane] = jnp.exp(g_last[pr, step]) * s_old - ms[:n] + bm[pr, step][:, :n]
            directions[d][5][0, rs, h * DN_HEAD_DIM:(h + 1) * DN_HEAD_DIM] = (
                ms[n:] + auw[pr][rs, :n]).astype(BF16)


def _dn_scan(q, k, v, gcol, gct):
    b, _, l, _ = q.shape
    nb = l // SCAN_BLOCK
    fwd4 = lambda bi, i: (bi, 0, i, 0)
    bwd4 = lambda bi, i: (bi, 0, nb - 1 - i, 0)
    fwd3 = lambda bi, i: (bi, i, 0)
    bwd3 = lambda bi, i: (bi, nb - 1 - i, 0)
    fwdt = lambda bi, i: (bi, i, 0, 0)
    bwdt = lambda bi, i: (bi, nb - 1 - i, 0, 0)
    qkv_spec = lambda imap: pl.BlockSpec((1, DN_HEADS, SCAN_BLOCK, DN_HEAD_DIM), imap)
    gate_spec = lambda imap: pl.BlockSpec((1, SCAN_BLOCK, V7X_LANES), imap)
    gct_spec = lambda imap: pl.BlockSpec((1, 1, 2 * DN_HEADS, SCAN_BLOCK), imap)
    out = jax.ShapeDtypeStruct((b, l, DN_WIDTH), BF16)
    return pl.pallas_call(
        _dn_scan_kernel,
        out_shape=[out, out],
        grid=(b, nb),
        in_specs=[qkv_spec(fwd4), qkv_spec(fwd4), qkv_spec(fwd4),
                  qkv_spec(bwd4), qkv_spec(bwd4), qkv_spec(bwd4),
                  gate_spec(fwd3), gct_spec(fwdt), gate_spec(bwd3), gct_spec(bwdt)],
        out_specs=[pl.BlockSpec((1, SCAN_BLOCK, DN_WIDTH), fwd3),
                   pl.BlockSpec((1, SCAN_BLOCK, DN_WIDTH), bwd3)],
        scratch_shapes=[pltpu.VMEM((2 * DN_HEADS, DN_HEAD_DIM, DN_HEAD_DIM), F32)],
        compiler_params=pltpu.CompilerParams(
            dimension_semantics=("arbitrary", "arbitrary"), vmem_limit_bytes=_vmem_limit(32 << 20)),
        name="dn_scan",
    )(q, k, v, q, k, v, gcol, gct, gcol, gct)


def _mix_out_kernel(alpha, oa_ref, of_ref, ob_ref, z_ref, x_ref, w_ref, ng_ref, g_ref, b_ref,
                    o_ref, dn_ref):
    for h in range(DN_HEADS):
        sl = slice(h * DN_HEAD_DIM, (h + 1) * DN_HEAD_DIM)
        o = of_ref[:, sl].astype(F32) + ob_ref[:, sl].astype(F32)
        ms = jnp.mean(o * o, axis=-1, keepdims=True)
        gate = _silu(z_ref[:, sl].astype(F32))
        dn_ref[:, sl] = (o * lax.rsqrt(ms + RMS_EPS) * ng_ref[...] * gate).astype(BF16)
    y = _dot(oa_ref[...], w_ref[:ATTN_WIDTH, :]) + _dot(dn_ref[...], w_ref[ATTN_WIDTH:, :])
    o_ref[...] = _layer_norm(alpha * x_ref[...] + y, g_ref[...], b_ref[...])


def _mix_out(oa, of, ob, z, x2, w_out, norm_gain, gain, bias, alpha):
    rows = x2.shape[0]
    row = lambda i: (i, 0)
    const = lambda i: (0, 0)
    return pl.pallas_call(
        functools.partial(_mix_out_kernel, alpha),
        out_shape=jax.ShapeDtypeStruct((rows, D_MODEL), F32),
        grid=(rows // ROW_TILE,),
        in_specs=[
            pl.BlockSpec((ROW_TILE, ATTN_WIDTH), row),
            pl.BlockSpec((ROW_TILE, DN_WIDTH), row),
            pl.BlockSpec((ROW_TILE, DN_WIDTH), row),
            pl.BlockSpec((ROW_TILE, DN_WIDTH), row),
            pl.BlockSpec((ROW_TILE, D_MODEL), row),
            pl.BlockSpec((D_MODEL, D_MODEL), const),
            pl.BlockSpec((1, DN_HEAD_DIM), const),
            pl.BlockSpec((1, D_MODEL), const),
            pl.BlockSpec((1, D_MODEL), const),
        ],
        out_specs=pl.BlockSpec((ROW_TILE, D_MODEL), row),
        scratch_shapes=[pltpu.VMEM((ROW_TILE, DN_WIDTH), BF16)],
        compiler_params=pltpu.CompilerParams(
            dimension_semantics=("arbitrary",), vmem_limit_bytes=_vmem_limit(40 << 20)),
        name="mix_out",
    )(oa, of, ob, z, x2, w_out, norm_gain, gain, bias)


def _pack_mixer_weights(w_in):
    aq = w_in[:, :ATTN_WIDTH]
    ak = w_in[:, ATTN_WIDTH:ATTN_WIDTH + KV_DIM]
    av = w_in[:, ATTN_WIDTH + KV_DIM:ATTN_WIDTH + 2 * KV_DIM]
    dqkv = w_in[:, ATTN_WIDTH + 2 * KV_DIM:ATTN_WIDTH + 2 * KV_DIM + 3 * DN_WIDTH]
    z = w_in[:, ATTN_WIDTH + 2 * KV_DIM + 3 * DN_WIDTH:OFF_B]
    gates = w_in[:, OFF_B:]
    w_row = jnp.concatenate([ak, z, dqkv], axis=1).astype(BF16)
    w_t = jnp.concatenate([aq, av, gates], axis=1).T.astype(BF16)
    return w_row, w_t


def _gate_params(a_log, dt_bias):
    cols = jnp.stack([a_log.reshape(-1), dt_bias.reshape(-1)], axis=1).astype(F32)
    return jnp.pad(cols, ((GATE_ROWS - cols.shape[0], 0), (0, V7X_LANES - cols.shape[1])))


def kernel(x_prompt, x_sample, ffn1_w_in, ffn1_w_out, w_in, conv_w, attn_sink, dn_a_log, dn_dt_bias,
           dn_norm_gain, w_out, ffn2_w_in, ffn2_w_out, ln_gain, ln_bias):
    depth = w_in.shape[0]
    alpha = (2.0 * depth) ** 0.25
    tri = _chunk_triangles()
    layers = []
    for i in range(depth):
        bias, sink = _attn_tables(attn_sink[i])
        w_row, w_t = _pack_mixer_weights(w_in[i])
        layers.append(dict(
            ffn1_in=ffn1_w_in[i].astype(BF16), ffn1_out=ffn1_w_out[i].astype(BF16),
            ffn2_in=ffn2_w_in[i].astype(BF16), ffn2_out=ffn2_w_out[i].astype(BF16),
            w_row=w_row, w_t=w_t, w_out=w_out[i].astype(BF16),
            conv=jnp.pad(conv_w[i].astype(F32), ((0, 8 - CONV_WIDTH), (0, 0))),
            gate=_gate_params(dn_a_log[i], dn_dt_bias[i]),
            norm_gain=dn_norm_gain[i].astype(F32).reshape(1, DN_HEAD_DIM),
            bias=bias, sink=sink,
            ln_g=ln_gain[i].astype(F32).reshape(3, 1, D_MODEL),
            ln_b=ln_bias[i].astype(F32).reshape(3, 1, D_MODEL),
        ))

    def trunk(x):
        b, l, _ = x.shape
        x2 = x.reshape(b * l, D_MODEL)
        for p in layers:
            x2 = _ffn_ln(x2, p["ffn1_in"], p["ffn1_out"], p["ln_g"][0], p["ln_b"][0], alpha)
            qt, ak, vt, z, dq, dk, dv, gcol, gct = _mixer_in(
                x2.reshape(b, l, D_MODEL), p["w_row"], p["w_t"], p["conv"], p["gate"], tri)
            oa = _attention(qt, ak, vt, p["bias"], p["sink"])
            of, ob = _dn_scan(dq, dk, dv, gcol, gct)
            x2 = _mix_out(oa.reshape(b * l, ATTN_WIDTH), of.reshape(b * l, DN_WIDTH),
                          ob.reshape(b * l, DN_WIDTH), z.reshape(b * l, DN_WIDTH), x2, p["w_out"],
                          p["norm_gain"], p["ln_g"][1], p["ln_b"][1], alpha)
            x2 = _ffn_ln(x2, p["ffn2_in"], p["ffn2_out"], p["ln_g"][2], p["ln_b"][2], alpha)
        return x2.reshape(b, l, D_MODEL)

    return (trunk(x_prompt), trunk(x_sample))
```

```python
import functools

import jax
import jax.numpy as jnp
from jax import lax
from jax.experimental import pallas as pl
from jax.experimental.pallas import tpu as pltpu

F32 = jnp.float32
BF16 = jnp.bfloat16

D_MODEL = 1024
ATTN_WIDTH = 512
ATTN_HEAD_DIM = 64
ATTN_HEADS = 8
ATTN_KV_HEADS = 2
ATTN_GROUP = ATTN_HEADS // ATTN_KV_HEADS
KV_DIM = ATTN_KV_HEADS * ATTN_HEAD_DIM
WINDOW = 128
BLOCK = 128
DN_WIDTH = 512
DN_HEAD_DIM = 128
DN_HEADS = DN_WIDTH // DN_HEAD_DIM
CONV_WIDTH = 5
CONV_PAD = CONV_WIDTH // 2
CHUNK = 64
D_FF = 2816
LN_EPS = 1e-5
RMS_EPS = 1e-6
OFF_B = ATTN_WIDTH + 2 * KV_DIM + 4 * DN_WIDTH
OFF_A = OFF_B + 2 * DN_HEADS

V7X_LANES = 128
V7X_BF16_SUBLANES = 16
V7X_VMEM_BYTES = 64 * 1024 * 1024

FFN_TILE = 1024
FFN_HALF = 512
FFN_CHUNK = 256
ATTN_QBLOCKS = 4
MIX_TILE = 512
MIX_HALO = V7X_BF16_SUBLANES
SCAN_BLOCK = 2 * CHUNK
SCAN_STEP_BLOCKS = 2
GATE_ROWS = 4 * DN_HEADS
NEG_BIG = -1e30


def _vmem_limit(nbytes):
    return int(min(nbytes, V7X_VMEM_BYTES - 4 * 1024 * 1024))


def _layer_norm(r, gain, bias):
    mu = jnp.mean(r, axis=-1, keepdims=True)
    c = r - mu
    var = jnp.mean(c * c, axis=-1, keepdims=True)
    return c * lax.rsqrt(var + LN_EPS) * gain + bias


def _silu(t):
    return t * jax.nn.sigmoid(t)


def _dot(a, b):
    return jnp.dot(a, b, preferred_element_type=F32)


def _swiglu(xb, win_ref, wout_ref, act_ref, rs):
    for j in range(D_FF // FFN_CHUNK):
        lo = j * FFN_CHUNK
        gate = _dot(xb, win_ref[:, lo:lo + FFN_CHUNK])
        up = _dot(xb, win_ref[:, D_FF + lo:D_FF + lo + FFN_CHUNK])
        act_ref[rs, lo:lo + FFN_CHUNK] = (_silu(gate) * up).astype(BF16)
    return _dot(act_ref[rs, :], wout_ref[...])


def _ffn_ln_kernel(alpha, x_ref, win_ref, wout_ref, g_ref, b_ref, o_ref, act_ref):
    for half in range(FFN_TILE // FFN_HALF):
        rs = slice(half * FFN_HALF, (half + 1) * FFN_HALF)
        x = x_ref[rs, :]
        y = _swiglu(x.astype(BF16), win_ref, wout_ref, act_ref, rs)
        o_ref[rs, :] = _layer_norm(alpha * x + 0.5 * y, g_ref[...], b_ref[...])


def _ffn_ln(x2, w_in, w_out, gain, bias, alpha):
    rows = x2.shape[0]
    const = lambda i: (0, 0)
    return pl.pallas_call(
        functools.partial(_ffn_ln_kernel, alpha),
        out_shape=jax.ShapeDtypeStruct((rows, D_MODEL), F32),
        grid=(rows // FFN_TILE,),
        in_specs=[
            pl.BlockSpec((FFN_TILE, D_MODEL), lambda i: (i, 0)),
            pl.BlockSpec((D_MODEL, 2 * D_FF), const, pipeline_mode=pl.Buffered(1)),
            pl.BlockSpec((D_FF, D_MODEL), const, pipeline_mode=pl.Buffered(1)),
            pl.BlockSpec((1, D_MODEL), const),
            pl.BlockSpec((1, D_MODEL), const),
        ],
        out_specs=pl.BlockSpec((FFN_TILE, D_MODEL), lambda i: (i, 0)),
        scratch_shapes=[pltpu.VMEM((FFN_TILE, D_FF), BF16)],
        compiler_params=pltpu.CompilerParams(
            dimension_semantics=("arbitrary",), vmem_limit_bytes=_vmem_limit(56 << 20)),
        name="ffn_ln",
    )(x2, w_in, w_out, gain, bias)


_W_ROW_K = 0
_W_ROW_Z = KV_DIM
_W_ROW_DQKV = KV_DIM + DN_WIDTH
_W_ROW_COLS = KV_DIM + DN_WIDTH + 3 * DN_WIDTH
_W_T_Q = 0
_W_T_V = ATTN_WIDTH
_W_T_GATE = ATTN_WIDTH + KV_DIM
_W_T_ROWS = _W_T_GATE + GATE_ROWS


def _mixer_in_kernel(x_ref, xp_ref, xn_ref, wrow_ref, wt_ref, conv_ref, gp_ref, tri_ref,
                     qt_ref, k_ref, vt_ref, z_ref, dq_ref, dk_ref, dv_ref, gcol_ref, gct_ref):
    i = pl.program_id(1)
    last = pl.num_programs(1) - 1
    tile, halo = MIX_TILE, MIX_HALO
    prev = jnp.where(i > 0, xp_ref[0], 0.0)
    nxt = jnp.where(i < last, xn_ref[0], 0.0)
    xe = jnp.concatenate([prev, x_ref[0], nxt], axis=0).astype(BF16)
    xb = xe[halo:halo + tile]

    tproj = lax.dot_general(wt_ref[...], xb, (((1,), (1,)), ((), ())), preferred_element_type=F32)
    qt_ref[0] = tproj[_W_T_Q:_W_T_V].astype(BF16)
    vt_ref[0] = tproj[_W_T_V:_W_T_GATE].astype(BF16)
    gate_logits = tproj[_W_T_GATE:_W_T_ROWS]

    k_ref[0] = _dot(xb, wrow_ref[:, _W_ROW_K:_W_ROW_Z]).astype(BF16)
    z_ref[0] = _dot(xb, wrow_ref[:, _W_ROW_Z:_W_ROW_DQKV]).astype(BF16)

    rows = tile + 2 * halo
    for kind, o_ref in enumerate((dq_ref, dk_ref, dv_ref)):
        lo = _W_ROW_DQKV + kind * DN_WIDTH
        proj = _dot(xe, wrow_ref[:, lo:lo + DN_WIDTH])
        for h in range(DN_HEADS):
            col = kind * DN_WIDTH + h * DN_HEAD_DIM
            xcol = proj[:, h * DN_HEAD_DIM:(h + 1) * DN_HEAD_DIM]
            y = jnp.zeros((tile, DN_HEAD_DIM), F32)
            for t in range(CONV_WIDTH):
                shift = (CONV_PAD - t) % rows
                xs = pltpu.roll(xcol, shift, axis=0) if shift else xcol
                y = y + conv_ref[t:t + 1, col:col + DN_HEAD_DIM] * xs[halo:halo + tile]
            s = _silu(y)
            if kind < 2:
                s = s * lax.rsqrt(jnp.sum(s * s, axis=-1, keepdims=True) + RMS_EPS)
            if kind == 0:
                s = s * (DN_HEAD_DIM ** -0.5)
            o_ref[0, h] = s.astype(BF16)

    row = lax.broadcasted_iota(jnp.int32, (GATE_ROWS, SCAN_BLOCK), 0)
    beta = jax.nn.sigmoid(gate_logits)
    t = gate_logits + gp_ref[:, 1:2]
    softplus = jnp.maximum(t, 0.0) + jnp.log1p(jnp.exp(-jnp.abs(t)))
    g = -jnp.exp(gp_ref[:, 0:1]) * softplus
    g_hi = g.astype(BF16)
    r1 = g - g_hi.astype(F32)
    g_mid = r1.astype(BF16)
    g_lo = (r1 - g_mid.astype(F32)).astype(BF16)
    is_fwd = row < 2 * DN_HEADS + DN_HEADS
    pad_rows = jnp.zeros((SCAN_BLOCK - GATE_ROWS, SCAN_BLOCK), F32)
    for r in range(tile // SCAN_BLOCK):
        ls = slice(r * SCAN_BLOCK, (r + 1) * SCAN_BLOCK)
        parts = _dot(jnp.concatenate([g_hi[:, ls], g_mid[:, ls], g_lo[:, ls]], axis=0), tri_ref[...])
        sums = parts[:GATE_ROWS] + parts[GATE_ROWS:2 * GATE_ROWS] + parts[2 * GATE_ROWS:]
        gc = jnp.where(is_fwd, sums[:, :SCAN_BLOCK], sums[:, SCAN_BLOCK:])
        gct_ref[0, r] = gc[2 * DN_HEADS:]
        both = jnp.where(row < 2 * DN_HEADS, beta[:, ls], gc)
        gcol_ref[0, ls, :] = jnp.concatenate([both, pad_rows], axis=0).T


def _mixer_in(x, w_row, w_t, conv_w, gate_params, tri):
    b, l, _ = x.shape
    per = MIX_TILE // MIX_HALO
    nh = l // MIX_HALO
    cur = lambda bi, i: (bi, i, 0)
    const2 = lambda bi, i: (0, 0)
    head_major = jax.ShapeDtypeStruct((b, DN_HEADS, l, DN_HEAD_DIM), BF16)
    head_spec = pl.BlockSpec((1, DN_HEADS, MIX_TILE, DN_HEAD_DIM), lambda bi, i: (bi, 0, i, 0))
    return pl.pallas_call(
        _mixer_in_kernel,
        out_shape=[
            jax.ShapeDtypeStruct((b, ATTN_WIDTH, l), BF16),
            jax.ShapeDtypeStruct((b, l, KV_DIM), BF16),
            jax.ShapeDtypeStruct((b, KV_DIM, l), BF16),
            jax.ShapeDtypeStruct((b, l, DN_WIDTH), BF16),
            head_major, head_major, head_major,
            jax.ShapeDtypeStruct((b, l, V7X_LANES), F32),
            jax.ShapeDtypeStruct((b, l // SCAN_BLOCK, 2 * DN_HEADS, SCAN_BLOCK), F32),
        ],
        grid=(b, l // MIX_TILE),
        in_specs=[
            pl.BlockSpec((1, MIX_TILE, D_MODEL), cur),
            pl.BlockSpec((1, MIX_HALO, D_MODEL), lambda bi, i: (bi, jnp.maximum(i * per - 1, 0), 0)),
            pl.BlockSpec((1, MIX_HALO, D_MODEL), lambda bi, i: (bi, jnp.minimum((i + 1) * per, nh - 1), 0)),
            pl.BlockSpec((D_MODEL, _W_ROW_COLS), const2, pipeline_mode=pl.Buffered(1)),
            pl.BlockSpec((_W_T_ROWS, D_MODEL), const2, pipeline_mode=pl.Buffered(1)),
            pl.BlockSpec((8, 3 * DN_WIDTH), const2),
            pl.BlockSpec((GATE_ROWS, V7X_LANES), const2),
            pl.BlockSpec((SCAN_BLOCK, 2 * SCAN_BLOCK), const2),
        ],
        out_specs=[
            pl.BlockSpec((1, ATTN_WIDTH, MIX_TILE), lambda bi, i: (bi, 0, i)),
            pl.BlockSpec((1, MIX_TILE, KV_DIM), cur),
            pl.BlockSpec((1, KV_DIM, MIX_TILE), lambda bi, i: (bi, 0, i)),
            pl.BlockSpec((1, MIX_TILE, DN_WIDTH), cur),
            head_spec, head_spec, head_spec,
            pl.BlockSpec((1, MIX_TILE, V7X_LANES), cur),
            pl.BlockSpec((1, MIX_TILE // SCAN_BLOCK, 2 * DN_HEADS, SCAN_BLOCK), lambda bi, i: (bi, i, 0, 0)),
        ],
        compiler_params=pltpu.CompilerParams(
            dimension_semantics=("arbitrary", "arbitrary"), vmem_limit_bytes=_vmem_limit(48 << 20)),
        name="mixer_in",
    )(x, x, x, w_row, w_t, conv_w, gate_params, tri)


def _chunk_triangles():
    t = jnp.arange(SCAN_BLOCK)
    same = (t[:, None] // CHUNK) == (t[None, :] // CHUNK)
    prefix = same & (t[:, None] <= t[None, :])
    suffix = same & (t[:, None] >= t[None, :])
    return jnp.concatenate([prefix, suffix], axis=1).astype(BF16)


def _attn_kernel(qt_ref, kp_ref, kc_ref, kn_ref, vtp_ref, vtc_ref, vtn_ref,
                 bias_first_ref, bias_mid_ref, bias_last_ref, sink_ref, o_ref):
    qt = qt_ref[0] * jnp.asarray(ATTN_HEAD_DIM ** -0.5, BF16)
    kcat = jnp.concatenate([kp_ref[0], kc_ref[0], kn_ref[0]], axis=0)
    vtcat = jnp.concatenate([vtp_ref[0], vtc_ref[0], vtn_ref[0]], axis=1)
    no_head = jnp.zeros((ATTN_HEAD_DIM, BLOCK), BF16)
    items = [(j, h) for j in range(ATTN_QBLOCKS) for h in range(ATTN_KV_HEADS)]
    scores, outs = {}, {}
    for j, h in items:
        cols = []
        for g in range(ATTN_GROUP):
            lo = (h * ATTN_GROUP + g) * ATTN_HEAD_DIM
            qhg = qt[lo:lo + ATTN_HEAD_DIM, j * BLOCK:(j + 1) * BLOCK]
            cols.append(jnp.concatenate([qhg, no_head] if h == 0 else [no_head, qhg], axis=0))
        scores[j, h] = _dot(kcat[j * BLOCK:(j + 3) * BLOCK], jnp.concatenate(cols, axis=1))
    for j, h in items:
        bias_ref = bias_first_ref if j == 0 else (bias_last_ref if j == ATTN_QBLOCKS - 1 else bias_mid_ref)
        logits = scores[j, h] + bias_ref[0, h]
        sink = sink_ref[h][0:1, :]
        m = jnp.maximum(jnp.max(logits, axis=0, keepdims=True), sink)
        p = jnp.exp(logits - m)
        denom = jnp.sum(p, axis=0, keepdims=True) + jnp.exp(sink - m)
        vt_h = vtcat[h * ATTN_HEAD_DIM:(h + 1) * ATTN_HEAD_DIM, j * BLOCK:(j + 3) * BLOCK]
        pv = _dot(vt_h, p.astype(BF16)) / denom
        outs[j, h] = [pv[:, g * BLOCK:(g + 1) * BLOCK] for g in range(ATTN_GROUP)]
    for j in range(ATTN_QBLOCKS):
        pieces = [piece for h in range(ATTN_KV_HEADS) for piece in outs[j, h]]
        o_ref[0, j * BLOCK:(j + 1) * BLOCK, :] = jnp.concatenate(pieces, axis=0).T.astype(BF16)


def _attention(qt, k, vt, bias, sink):
    b, _, l = qt.shape
    qb = ATTN_QBLOCKS
    nb = l // BLOCK
    steps = nb // qb
    before = lambda n: jnp.maximum(n * qb - 1, 0)
    after = lambda n: jnp.minimum((n + 1) * qb, nb - 1)
    no_prev = lambda n: (n == 0).astype(jnp.int32)
    no_next = lambda n: 2 * (n == steps - 1).astype(jnp.int32)
    first = lambda bi, n: (no_prev(n) + (no_next(n) if qb == 1 else 0), 0, 0, 0)
    mid = lambda bi, n: (0, 0, 0, 0)
    last = lambda bi, n: (no_next(n), 0, 0, 0)
    k_halo = lambda f: pl.BlockSpec((1, BLOCK, KV_DIM), lambda bi, n: (bi, f(n), 0))
    vt_halo = lambda f: pl.BlockSpec((1, KV_DIM, BLOCK), lambda bi, n: (bi, 0, f(n)))
    bias_spec = lambda imap: pl.BlockSpec((1, ATTN_KV_HEADS, 3 * BLOCK, ATTN_GROUP * BLOCK), imap)
    return pl.pallas_call(
        _attn_kernel,
        out_shape=jax.ShapeDtypeStruct((b, l, ATTN_WIDTH), BF16),
        grid=(b, steps),
        in_specs=[
            pl.BlockSpec((1, ATTN_WIDTH, qb * BLOCK), lambda bi, n: (bi, 0, n)),
            k_halo(before), pl.BlockSpec((1, qb * BLOCK, KV_DIM), lambda bi, n: (bi, n, 0)), k_halo(after),
            vt_halo(before), pl.BlockSpec((1, KV_DIM, qb * BLOCK), lambda bi, n: (bi, 0, n)), vt_halo(after),
            bias_spec(first), bias_spec(mid), bias_spec(last),
            pl.BlockSpec((ATTN_KV_HEADS, 8, ATTN_GROUP * BLOCK), lambda bi, n: (0, 0, 0)),
        ],
        out_specs=pl.BlockSpec((1, qb * BLOCK, ATTN_WIDTH), lambda bi, n: (bi, n, 0)),
        compiler_params=pltpu.CompilerParams(
            dimension_semantics=("arbitrary", "arbitrary"), vmem_limit_bytes=_vmem_limit(40 << 20)),
        name="swa_attention",
    )(qt, k, k, k, vt, vt, vt, bias, bias, bias, sink)


def _attn_tables(attn_sink):
    slopes = 2.0 ** (-8.0 * jnp.arange(1, ATTN_HEADS + 1, dtype=F32) / ATTN_HEADS)
    kj = jnp.arange(3 * BLOCK)[:, None]
    qi = jnp.arange(BLOCK)[None, :]
    dist = jnp.abs(qi - kj + BLOCK)
    bias = jnp.where(dist <= WINDOW, -slopes[:, None, None] * dist.astype(F32)[None], NEG_BIG)
    bias = bias.reshape(ATTN_KV_HEADS, ATTN_GROUP, 3 * BLOCK, BLOCK).transpose(0, 2, 1, 3)
    bias = bias.reshape(ATTN_KV_HEADS, 3 * BLOCK, ATTN_GROUP * BLOCK)
    no_prev = (kj < BLOCK)[None]
    no_next = (kj >= 2 * BLOCK)[None]
    bias = jnp.stack([jnp.where(hidden, NEG_BIG, bias)
                      for hidden in (jnp.zeros_like(no_prev), no_prev, no_next, no_prev | no_next)])
    sink = jnp.broadcast_to(attn_sink.astype(F32).reshape(ATTN_KV_HEADS, 1, ATTN_GROUP, 1),
                            (ATTN_KV_HEADS, 8, ATTN_GROUP, BLOCK))
    return bias, sink.reshape(ATTN_KV_HEADS, 8, ATTN_GROUP * BLOCK)


def _dn_scan_kernel(qf_ref, kf_ref, vf_ref, qb_ref, kb_ref, vb_ref,
                    gcolf_ref, gctf_ref, gcolb_ref, gctb_ref,
                    of_ref, ob_ref, state_ref):
    @pl.when(pl.program_id(1) == 0)
    def _():
        state_ref[...] = jnp.zeros_like(state_ref)

    n = SCAN_BLOCK
    row = lax.broadcasted_iota(jnp.int32, (n, n), 0)
    col = lax.broadcasted_iota(jnp.int32, (n, n), 1)
    same_chunk = (row // CHUNK) == (col // CHUNK)
    eye = (row == col).astype(F32)
    off_diag = row != col
    directions = (
        (qf_ref, kf_ref, vf_ref, gcolf_ref, gctf_ref, of_ref, same_chunk & (row >= col)),
        (qb_ref, kb_ref, vb_ref, gcolb_ref, gctb_ref, ob_ref, same_chunk & (row <= col)),
    )
    probs = [(d, h, b) for b in range(SCAN_STEP_BLOCKS) for d in range(2) for h in range(DN_HEADS)]
    kk, qq, vv, beta, gcol, egc, p, y, attn = ({} for _ in range(9))
    for pr in probs:
        d, h, b = pr
        q_ref, k_ref, v_ref, gcol_ref, gct_ref, _, incl = directions[d]
        lane = d * DN_HEADS + h
        blk = slice(b * n, (b + 1) * n)
        kk[pr] = k_ref[0, h, blk]
        qq[pr] = q_ref[0, h, blk]
        vv[pr] = v_ref[0, h, blk]
        beta[pr] = gcol_ref[0, blk][:, lane:lane + 1]
        gcol[pr] = gcol_ref[0, blk][:, 2 * DN_HEADS + lane:2 * DN_HEADS + lane + 1]
        grow = gct_ref[0, b][lane:lane + 1, :]
        gram = lax.dot_general(jnp.concatenate([kk[pr], qq[pr]], axis=0), kk[pr],
                               (((1,), (1,)), ((), ())), preferred_element_type=F32)
        decay = jnp.exp(jnp.where(incl, gcol[pr] - grow, NEG_BIG))
        neg_n = -(beta[pr] * gram[:n] * jnp.where(off_diag, decay, 0.0))
        attn[pr] = (gram[n:] * decay).astype(BF16)
        p[pr] = eye + neg_n
        y[pr] = neg_n.astype(BF16)
    for pr in probs:
        y[pr] = _dot(y[pr], y[pr])
    for _ in range(4):
        for pr in probs:
            yb = y[pr].astype(BF16)
            r = _dot(yb, jnp.concatenate([p[pr].astype(BF16), yb], axis=1))
            p[pr] = p[pr] + r[:, :n]
            y[pr] = r[:, n:]
    for pr in probs:
        p[pr] = p[pr] + _dot(y[pr].astype(BF16), p[pr].astype(BF16))
    uwb, auw, q_eff = {}, {}, {}
    for pr in probs:
        egc[pr] = jnp.exp(gcol[pr])
        rhs = jnp.concatenate([vv[pr].astype(F32) * beta[pr],
                               kk[pr].astype(F32) * (beta[pr] * egc[pr])], axis=1)
        uwb[pr] = _dot(p[pr].astype(BF16), rhs.astype(BF16)).astype(BF16)
    for pr in probs:
        auw[pr] = _dot(attn[pr], uwb[pr])
        q_eff[pr] = qq[pr].astype(F32) * egc[pr] - auw[pr][:, n:]
    chunks_per_block = n // CHUNK
    steps = SCAN_STEP_BLOCKS * chunks_per_block

    def locate(d, step):
        pos = step if d == 0 else steps - 1 - step
        return divmod(pos, chunks_per_block)

    bm, g_last = {}, {}
    for step in range(steps):
        for d in range(2):
            for h in range(DN_HEADS):
                b, c = locate(d, step)
                pr = (d, h, b)
                rs = slice(c * CHUNK, (c + 1) * CHUNK)
                end = c * CHUNK + (CHUNK - 1 if d == 0 else 0)
                g_last[pr, c] = gcol[pr][end:end + 1]
                k_dec = kk[pr][rs].astype(F32) * jnp.exp(g_last[pr, c] - gcol[pr][rs])
                bm[pr, c] = lax.dot_general(k_dec.astype(BF16), uwb[pr][rs], (((0,), (0,)), ((), ())),
                                            preferred_element_type=F32)
    for step in range(steps):
        for d in range(2):
            o_ref = directions[d][5]
            for h in range(DN_HEADS):
                b, c = locate(d, step)
                pr = (d, h, b)
                rs = slice(c * CHUNK, (c + 1) * CHUNK)
                lane = d * DN_HEADS + h
                s_old = state_ref[lane]
                lhs = jnp.concatenate([bm[pr, c][:, n:], q_eff[pr][rs]], axis=0).astype(BF16)
                ms = _dot(lhs, s_old.astype(BF16))
                state_ref[lane] = jnp.exp(g_last[pr, c]) * s_old - ms[:n] + bm[pr, c][:, :n]
                o_ref[0, b * n + c * CHUNK:b * n + (c + 1) * CHUNK, h * DN_HEAD_DIM:(h + 1) * DN_HEAD_DIM] = (
                    ms[n:] + auw[pr][rs, :n]).astype(BF16)


def _dn_scan(q, k, v, gcol, gct):
    b, _, l, _ = q.shape
    span = SCAN_STEP_BLOCKS * SCAN_BLOCK
    nb = l // span
    fwd4 = lambda bi, i: (bi, 0, i, 0)
    bwd4 = lambda bi, i: (bi, 0, nb - 1 - i, 0)
    fwd3 = lambda bi, i: (bi, i, 0)
    bwd3 = lambda bi, i: (bi, nb - 1 - i, 0)
    fwdt = lambda bi, i: (bi, i, 0, 0)
    bwdt = lambda bi, i: (bi, nb - 1 - i, 0, 0)
    qkv_spec = lambda imap: pl.BlockSpec((1, DN_HEADS, span, DN_HEAD_DIM), imap)
    gate_spec = lambda imap: pl.BlockSpec((1, span, V7X_LANES), imap)
    gct_spec = lambda imap: pl.BlockSpec((1, SCAN_STEP_BLOCKS, 2 * DN_HEADS, SCAN_BLOCK), imap)
    out = jax.ShapeDtypeStruct((b, l, DN_WIDTH), BF16)
    return pl.pallas_call(
        _dn_scan_kernel,
        out_shape=[out, out],
        grid=(b, nb),
        in_specs=[qkv_spec(fwd4), qkv_spec(fwd4), qkv_spec(fwd4),
                  qkv_spec(bwd4), qkv_spec(bwd4), qkv_spec(bwd4),
                  gate_spec(fwd3), gct_spec(fwdt), gate_spec(bwd3), gct_spec(bwdt)],
        out_specs=[pl.BlockSpec((1, span, DN_WIDTH), fwd3),
                   pl.BlockSpec((1, span, DN_WIDTH), bwd3)],
        scratch_shapes=[pltpu.VMEM((2 * DN_HEADS, DN_HEAD_DIM, DN_HEAD_DIM), F32)],
        compiler_params=pltpu.CompilerParams(
            dimension_semantics=("arbitrary", "arbitrary"), vmem_limit_bytes=_vmem_limit(40 << 20)),
        name="dn_scan",
    )(q, k, v, q, k, v, gcol, gct, gcol, gct)


def _mix_ffn_kernel(alpha, oa_ref, of_ref, ob_ref, z_ref, x_ref, w_ref, ng_ref, g2_ref, b2_ref,
                    win_ref, wout_ref, g3_ref, b3_ref, o_ref, dn_ref, act_ref):
    for half in range(FFN_TILE // FFN_HALF):
        rs = slice(half * FFN_HALF, (half + 1) * FFN_HALF)
        for h in range(DN_HEADS):
            sl = slice(h * DN_HEAD_DIM, (h + 1) * DN_HEAD_DIM)
            o = of_ref[rs, sl].astype(F32) + ob_ref[rs, sl].astype(F32)
            ms = jnp.mean(o * o, axis=-1, keepdims=True)
            gate = _silu(z_ref[rs, sl].astype(F32))
            dn_ref[rs, sl] = (o * lax.rsqrt(ms + RMS_EPS) * ng_ref[...] * gate).astype(BF16)
        y = _dot(oa_ref[rs, :], w_ref[:ATTN_WIDTH, :]) + _dot(dn_ref[rs, :], w_ref[ATTN_WIDTH:, :])
        x = _layer_norm(alpha * x_ref[rs, :] + y, g2_ref[...], b2_ref[...])
        y = _swiglu(x.astype(BF16), win_ref, wout_ref, act_ref, rs)
        o_ref[rs, :] = _layer_norm(alpha * x + 0.5 * y, g3_ref[...], b3_ref[...])


def _mix_ffn(oa, of, ob, z, x2, w_out, norm_gain, gain2, bias2, w_in, w_ffn_out, gain3, bias3, alpha):
    rows = x2.shape[0]
    row = lambda i: (i, 0)
    const = lambda i: (0, 0)
    resident = lambda shape: pl.BlockSpec(shape, const, pipeline_mode=pl.Buffered(1))
    return pl.pallas_call(
        functools.partial(_mix_ffn_kernel, alpha),
        out_shape=jax.ShapeDtypeStruct((rows, D_MODEL), F32),
        grid=(rows // FFN_TILE,),
        in_specs=[
            pl.BlockSpec((FFN_TILE, ATTN_WIDTH), row),
            pl.BlockSpec((FFN_TILE, DN_WIDTH), row),
            pl.BlockSpec((FFN_TILE, DN_WIDTH), row),
            pl.BlockSpec((FFN_TILE, DN_WIDTH), row),
            pl.BlockSpec((FFN_TILE, D_MODEL), row),
            resident((D_MODEL, D_MODEL)),
            pl.BlockSpec((1, DN_HEAD_DIM), const),
            pl.BlockSpec((1, D_MODEL), const),
            pl.BlockSpec((1, D_MODEL), const),
            resident((D_MODEL, 2 * D_FF)),
            resident((D_FF, D_MODEL)),
            pl.BlockSpec((1, D_MODEL), const),
            pl.BlockSpec((1, D_MODEL), const),
        ],
        out_specs=pl.BlockSpec((FFN_TILE, D_MODEL), row),
        scratch_shapes=[pltpu.VMEM((FFN_TILE, DN_WIDTH), BF16), pltpu.VMEM((FFN_TILE, D_FF), BF16)],
        compiler_params=pltpu.CompilerParams(
            dimension_semantics=("arbitrary",), vmem_limit_bytes=_vmem_limit(60 << 20)),
        name="mix_ffn",
    )(oa, of, ob, z, x2, w_out, norm_gain, gain2, bias2, w_in, w_ffn_out, gain3, bias3)


def _pack_mixer_weights(w_in):
    aq = w_in[:, :ATTN_WIDTH]
    ak = w_in[:, ATTN_WIDTH:ATTN_WIDTH + KV_DIM]
    av = w_in[:, ATTN_WIDTH + KV_DIM:ATTN_WIDTH + 2 * KV_DIM]
    dqkv = w_in[:, ATTN_WIDTH + 2 * KV_DIM:ATTN_WIDTH + 2 * KV_DIM + 3 * DN_WIDTH]
    z = w_in[:, ATTN_WIDTH + 2 * KV_DIM + 3 * DN_WIDTH:OFF_B]
    gates = w_in[:, OFF_B:]
    w_row = jnp.concatenate([ak, z, dqkv], axis=1).astype(BF16)
    w_t = jnp.concatenate([aq, av, gates], axis=1).T.astype(BF16)
    return w_row, w_t


def _gate_params(a_log, dt_bias):
    cols = jnp.stack([a_log.reshape(-1), dt_bias.reshape(-1)], axis=1).astype(F32)
    return jnp.pad(cols, ((GATE_ROWS - cols.shape[0], 0), (0, V7X_LANES - cols.shape[1])))


def kernel(x_prompt, x_sample, ffn1_w_in, ffn1_w_out, w_in, conv_w, attn_sink, dn_a_log, dn_dt_bias,
           dn_norm_gain, w_out, ffn2_w_in, ffn2_w_out, ln_gain, ln_bias):
    depth = w_in.shape[0]
    alpha = (2.0 * depth) ** 0.25
    tri = _chunk_triangles()
    layers = []
    for i in range(depth):
        bias, sink = _attn_tables(attn_sink[i])
        w_row, w_t = _pack_mixer_weights(w_in[i])
        layers.append(dict(
            ffn1_in=ffn1_w_in[i].astype(BF16), ffn1_out=ffn1_w_out[i].astype(BF16),
            ffn2_in=ffn2_w_in[i].astype(BF16), ffn2_out=ffn2_w_out[i].astype(BF16),
            w_row=w_row, w_t=w_t, w_out=w_out[i].astype(BF16),
            conv=jnp.pad(conv_w[i].astype(F32), ((0, 8 - CONV_WIDTH), (0, 0))),
            gate=_gate_params(dn_a_log[i], dn_dt_bias[i]),
            norm_gain=dn_norm_gain[i].astype(F32).reshape(1, DN_HEAD_DIM),
            bias=bias, sink=sink,
            ln_g=ln_gain[i].astype(F32).reshape(3, 1, D_MODEL),
            ln_b=ln_bias[i].astype(F32).reshape(3, 1, D_MODEL),
        ))

    def trunk(x):
        b, l, _ = x.shape
        x2 = x.reshape(b * l, D_MODEL)
        for p in layers:
            x2 = _ffn_ln(x2, p["ffn1_in"], p["ffn1_out"], p["ln_g"][0], p["ln_b"][0], alpha)
            qt, ak, vt, z, dq, dk, dv, gcol, gct = _mixer_in(
                x2.reshape(b, l, D_MODEL), p["w_row"], p["w_t"], p["conv"], p["gate"], tri)
            oa = _attention(qt, ak, vt, p["bias"], p["sink"])
            of, ob = _dn_scan(dq, dk, dv, gcol, gct)
            x2 = _mix_ffn(oa.reshape(b * l, ATTN_WIDTH), of.reshape(b * l, DN_WIDTH),
                          ob.reshape(b * l, DN_WIDTH), z.reshape(b * l, DN_WIDTH), x2, p["w_out"],
                          p["norm_gain"], p["ln_g"][1], p["ln_b"][1],
                          p["ffn2_in"], p["ffn2_out"], p["ln_g"][2], p["ln_b"][2], alpha)
        return x2.reshape(b, l, D_MODEL)

    return (trunk(x_prompt), trunk(x_sample))
```

```python
import functools

import jax
import jax.numpy as jnp
from jax import lax
from jax.experimental import pallas as pl
from jax.experimental.pallas import tpu as pltpu

F32 = jnp.float32
BF16 = jnp.bfloat16

D_MODEL = 1024
ATTN_WIDTH = 512
ATTN_HEAD_DIM = 64
ATTN_HEADS = 8
ATTN_KV_HEADS = 2
ATTN_GROUP = ATTN_HEADS // ATTN_KV_HEADS
KV_DIM = ATTN_KV_HEADS * ATTN_HEAD_DIM
WINDOW = 128
BLOCK = 128
DN_WIDTH = 512
DN_HEAD_DIM = 128
DN_HEADS = DN_WIDTH // DN_HEAD_DIM
CONV_WIDTH = 5
CONV_PAD = CONV_WIDTH // 2
CHUNK = 64
D_FF = 2816
LN_EPS = 1e-5
RMS_EPS = 1e-6
OFF_B = ATTN_WIDTH + 2 * KV_DIM + 4 * DN_WIDTH
OFF_A = OFF_B + 2 * DN_HEADS

V7X_LANES = 128
V7X_BF16_SUBLANES = 16
V7X_VMEM_BYTES = 64 * 1024 * 1024

FFN_TILE = 1024
FFN_PART = 256
MIX_FFN_PART = 512
FFN_CHUNK = 256
ATTN_QBLOCKS = 4
MIX_TILE = 512
MIX_HALO = V7X_BF16_SUBLANES
SCAN_BLOCK = 2 * CHUNK
SCAN_STEP_BLOCKS = 2
GATE_ROWS = 4 * DN_HEADS
NEG_BIG = -1e30


def _vmem_limit(nbytes):
    return int(min(nbytes, V7X_VMEM_BYTES - 4 * 1024 * 1024))


def _layer_norm(r, gain, bias):
    mu = jnp.mean(r, axis=-1, keepdims=True)
    c = r - mu
    var = jnp.mean(c * c, axis=-1, keepdims=True)
    return c * lax.rsqrt(var + LN_EPS) * gain + bias


def _silu(t):
    return t * jax.nn.sigmoid(t)


def _dot(a, b):
    return jnp.dot(a, b, preferred_element_type=F32)


def _swiglu(xb, win_ref, wout_ref, act_ref, rs):
    for j in range(D_FF // FFN_CHUNK):
        lo = j * FFN_CHUNK
        gate = _dot(xb, win_ref[:, lo:lo + FFN_CHUNK])
        up = _dot(xb, win_ref[:, D_FF + lo:D_FF + lo + FFN_CHUNK])
        act_ref[rs, lo:lo + FFN_CHUNK] = (_silu(gate) * up).astype(BF16)
    return _dot(act_ref[rs, :], wout_ref[...])


def _ffn_ln_kernel(alpha, x_ref, win_ref, wout_ref, g_ref, b_ref, o_ref, act_ref):
    for part in range(FFN_TILE // FFN_PART):
        rs = slice(part * FFN_PART, (part + 1) * FFN_PART)
        x = x_ref[rs, :]
        y = _swiglu(x.astype(BF16), win_ref, wout_ref, act_ref, rs)
        o_ref[rs, :] = _layer_norm(alpha * x + 0.5 * y, g_ref[...], b_ref[...])


def _ffn_ln(x2, w_in, w_out, gain, bias, alpha):
    rows = x2.shape[0]
    const = lambda i: (0, 0)
    return pl.pallas_call(
        functools.partial(_ffn_ln_kernel, alpha),
        out_shape=jax.ShapeDtypeStruct((rows, D_MODEL), F32),
        grid=(rows // FFN_TILE,),
        in_specs=[
            pl.BlockSpec((FFN_TILE, D_MODEL), lambda i: (i, 0)),
            pl.BlockSpec((D_MODEL, 2 * D_FF), const, pipeline_mode=pl.Buffered(1)),
            pl.BlockSpec((D_FF, D_MODEL), const, pipeline_mode=pl.Buffered(1)),
            pl.BlockSpec((1, D_MODEL), const),
            pl.BlockSpec((1, D_MODEL), const),
        ],
        out_specs=pl.BlockSpec((FFN_TILE, D_MODEL), lambda i: (i, 0)),
        scratch_shapes=[pltpu.VMEM((FFN_TILE, D_FF), BF16)],
        compiler_params=pltpu.CompilerParams(
            dimension_semantics=("arbitrary",), vmem_limit_bytes=_vmem_limit(56 << 20)),
        name="ffn_ln",
    )(x2, w_in, w_out, gain, bias)


_W_ROW_K = 0
_W_ROW_Z = KV_DIM
_W_ROW_DQKV = KV_DIM + DN_WIDTH
_W_ROW_COLS = KV_DIM + DN_WIDTH + 3 * DN_WIDTH
_W_T_Q = 0
_W_T_V = ATTN_WIDTH
_W_T_GATE = ATTN_WIDTH + KV_DIM
_W_T_ROWS = _W_T_GATE + GATE_ROWS


def _mixer_in_kernel(x_ref, xp_ref, xn_ref, wrow_ref, wt_ref, conv_ref, gp_ref, tri_ref,
                     qt_ref, k_ref, vt_ref, z_ref, dq_ref, dk_ref, dv_ref, gcol_ref, gct_ref, proj_ref):
    i = pl.program_id(1)
    last = pl.num_programs(1) - 1
    tile, halo = MIX_TILE, MIX_HALO
    prev = jnp.where(i > 0, xp_ref[0], 0.0)
    nxt = jnp.where(i < last, xn_ref[0], 0.0)
    xe = jnp.concatenate([prev, x_ref[0], nxt], axis=0).astype(BF16)
    xb = xe[halo:halo + tile]

    tproj = lax.dot_general(wt_ref[...], xb, (((1,), (1,)), ((), ())), preferred_element_type=F32)
    qt_ref[0] = tproj[_W_T_Q:_W_T_V].astype(BF16)
    vt_ref[0] = tproj[_W_T_V:_W_T_GATE].astype(BF16)
    gate_logits = tproj[_W_T_GATE:_W_T_ROWS]

    k_ref[0] = _dot(xb, wrow_ref[:, _W_ROW_K:_W_ROW_Z]).astype(BF16)
    z_ref[0] = _dot(xb, wrow_ref[:, _W_ROW_Z:_W_ROW_DQKV]).astype(BF16)

    for kind, o_ref in enumerate((dq_ref, dk_ref, dv_ref)):
        lo = _W_ROW_DQKV + kind * DN_WIDTH
        proj = _dot(xe, wrow_ref[:, lo:lo + DN_WIDTH])
        for h in range(DN_HEADS):
            proj_ref[h] = proj[:, h * DN_HEAD_DIM:(h + 1) * DN_HEAD_DIM]
        for h in range(DN_HEADS):
            col = kind * DN_WIDTH + h * DN_HEAD_DIM
            y = jnp.zeros((tile, DN_HEAD_DIM), F32)
            for t in range(CONV_WIDTH):
                first = halo + t - CONV_PAD
                xs = proj_ref[h, pl.ds(first, tile), :]
                y = y + conv_ref[t:t + 1, col:col + DN_HEAD_DIM] * xs
            s = _silu(y)
            if kind < 2:
                s = s * lax.rsqrt(jnp.sum(s * s, axis=-1, keepdims=True) + RMS_EPS)
            if kind == 0:
                s = s * (DN_HEAD_DIM ** -0.5)
            o_ref[0, h] = s.astype(BF16)

    row = lax.broadcasted_iota(jnp.int32, (GATE_ROWS, SCAN_BLOCK), 0)
    beta = jax.nn.sigmoid(gate_logits)
    t = gate_logits + gp_ref[:, 1:2]
    softplus = jnp.maximum(t, 0.0) + jnp.log1p(jnp.exp(-jnp.abs(t)))
    g = -jnp.exp(gp_ref[:, 0:1]) * softplus
    g_hi = g.astype(BF16)
    r1 = g - g_hi.astype(F32)
    g_mid = r1.astype(BF16)
    g_lo = (r1 - g_mid.astype(F32)).astype(BF16)
    is_fwd = row < 2 * DN_HEADS + DN_HEADS
    pad_rows = jnp.zeros((SCAN_BLOCK - GATE_ROWS, SCAN_BLOCK), F32)
    for r in range(tile // SCAN_BLOCK):
        ls = slice(r * SCAN_BLOCK, (r + 1) * SCAN_BLOCK)
        parts = _dot(jnp.concatenate([g_hi[:, ls], g_mid[:, ls], g_lo[:, ls]], axis=0), tri_ref[...])
        sums = parts[:GATE_ROWS] + parts[GATE_ROWS:2 * GATE_ROWS] + parts[2 * GATE_ROWS:]
        gc = jnp.where(is_fwd, sums[:, :SCAN_BLOCK], sums[:, SCAN_BLOCK:])
        gct_ref[0, r] = gc[2 * DN_HEADS:]
        both = jnp.where(row < 2 * DN_HEADS, beta[:, ls], gc)
        gcol_ref[0, ls, :] = jnp.concatenate([both, pad_rows], axis=0).T


def _mixer_in(x, w_row, w_t, conv_w, gate_params, tri):
    b, l, _ = x.shape
    per = MIX_TILE // MIX_HALO
    nh = l // MIX_HALO
    cur = lambda bi, i: (bi, i, 0)
    const2 = lambda bi, i: (0, 0)
    head_major = jax.ShapeDtypeStruct((b, DN_HEADS, l, DN_HEAD_DIM), BF16)
    head_spec = pl.BlockSpec((1, DN_HEADS, MIX_TILE, DN_HEAD_DIM), lambda bi, i: (bi, 0, i, 0))
    return pl.pallas_call(
        _mixer_in_kernel,
        out_shape=[
            jax.ShapeDtypeStruct((b, ATTN_WIDTH, l), BF16),
            jax.ShapeDtypeStruct((b, l, KV_DIM), BF16),
            jax.ShapeDtypeStruct((b, KV_DIM, l), BF16),
            jax.ShapeDtypeStruct((b, l, DN_WIDTH), BF16),
            head_major, head_major, head_major,
            jax.ShapeDtypeStruct((b, l, V7X_LANES), F32),
            jax.ShapeDtypeStruct((b, l // SCAN_BLOCK, 2 * DN_HEADS, SCAN_BLOCK), F32),
        ],
        grid=(b, l // MIX_TILE),
        in_specs=[
            pl.BlockSpec((1, MIX_TILE, D_MODEL), cur),
            pl.BlockSpec((1, MIX_HALO, D_MODEL), lambda bi, i: (bi, jnp.maximum(i * per - 1, 0), 0)),
            pl.BlockSpec((1, MIX_HALO, D_MODEL), lambda bi, i: (bi, jnp.minimum((i + 1) * per, nh - 1), 0)),
            pl.BlockSpec((D_MODEL, _W_ROW_COLS), const2, pipeline_mode=pl.Buffered(1)),
            pl.BlockSpec((_W_T_ROWS, D_MODEL), const2, pipeline_mode=pl.Buffered(1)),
            pl.BlockSpec((8, 3 * DN_WIDTH), const2),
            pl.BlockSpec((GATE_ROWS, V7X_LANES), const2),
            pl.BlockSpec((SCAN_BLOCK, 2 * SCAN_BLOCK), const2),
        ],
        out_specs=[
            pl.BlockSpec((1, ATTN_WIDTH, MIX_TILE), lambda bi, i: (bi, 0, i)),
            pl.BlockSpec((1, MIX_TILE, KV_DIM), cur),
            pl.BlockSpec((1, KV_DIM, MIX_TILE), lambda bi, i: (bi, 0, i)),
            pl.BlockSpec((1, MIX_TILE, DN_WIDTH), cur),
            head_spec, head_spec, head_spec,
            pl.BlockSpec((1, MIX_TILE, V7X_LANES), cur),
            pl.BlockSpec((1, MIX_TILE // SCAN_BLOCK, 2 * DN_HEADS, SCAN_BLOCK), lambda bi, i: (bi, i, 0, 0)),
        ],
        scratch_shapes=[pltpu.VMEM((DN_HEADS, MIX_TILE + 2 * MIX_HALO, DN_HEAD_DIM), F32)],
        compiler_params=pltpu.CompilerParams(
            dimension_semantics=("arbitrary", "arbitrary"), vmem_limit_bytes=_vmem_limit(48 << 20)),
        name="mixer_in",
    )(x, x, x, w_row, w_t, conv_w, gate_params, tri)


def _chunk_triangles():
    t = jnp.arange(SCAN_BLOCK)
    same = (t[:, None] // CHUNK) == (t[None, :] // CHUNK)
    prefix = same & (t[:, None] <= t[None, :])
    suffix = same & (t[:, None] >= t[None, :])
    return jnp.concatenate([prefix, suffix], axis=1).astype(BF16)


def _attn_kernel(qt_ref, kp_ref, kc_ref, kn_ref, vtp_ref, vtc_ref, vtn_ref,
                 bias_first_ref, bias_mid_ref, bias_last_ref, sink_ref, o_ref):
    qt = qt_ref[0] * jnp.asarray(ATTN_HEAD_DIM ** -0.5, BF16)
    kcat = jnp.concatenate([kp_ref[0], kc_ref[0], kn_ref[0]], axis=0)
    vtcat = jnp.concatenate([vtp_ref[0], vtc_ref[0], vtn_ref[0]], axis=1)
    no_head = jnp.zeros((ATTN_HEAD_DIM, BLOCK), BF16)
    items = [(j, h) for j in range(ATTN_QBLOCKS) for h in range(ATTN_KV_HEADS)]
    scores, outs = {}, {}
    for j, h in items:
        cols = []
        for g in range(ATTN_GROUP):
            lo = (h * ATTN_GROUP + g) * ATTN_HEAD_DIM
            qhg = qt[lo:lo + ATTN_HEAD_DIM, j * BLOCK:(j + 1) * BLOCK]
            cols.append(jnp.concatenate([qhg, no_head] if h == 0 else [no_head, qhg], axis=0))
        scores[j, h] = _dot(kcat[j * BLOCK:(j + 3) * BLOCK], jnp.concatenate(cols, axis=1))
    for j, h in items:
        bias_ref = bias_first_ref if j == 0 else (bias_last_ref if j == ATTN_QBLOCKS - 1 else bias_mid_ref)
        logits = scores[j, h] + bias_ref[0, h]
        sink = sink_ref[h][0:1, :]
        m = jnp.maximum(jnp.max(logits, axis=0, keepdims=True), sink)
        p = jnp.exp(logits - m)
        denom = jnp.sum(p, axis=0, keepdims=True) + jnp.exp(sink - m)
        vt_h = vtcat[h * ATTN_HEAD_DIM:(h + 1) * ATTN_HEAD_DIM, j * BLOCK:(j + 3) * BLOCK]
        pv = _dot(vt_h, p.astype(BF16)) / denom
        outs[j, h] = [pv[:, g * BLOCK:(g + 1) * BLOCK] for g in range(ATTN_GROUP)]
    for j in range(ATTN_QBLOCKS):
        pieces = [piece for h in range(ATTN_KV_HEADS) for piece in outs[j, h]]
        o_ref[0, j * BLOCK:(j + 1) * BLOCK, :] = jnp.concatenate(pieces, axis=0).T.astype(BF16)


def _attention(qt, k, vt, bias, sink):
    b, _, l = qt.shape
    qb = ATTN_QBLOCKS
    nb = l // BLOCK
    steps = nb // qb
    before = lambda n: jnp.maximum(n * qb - 1, 0)
    after = lambda n: jnp.minimum((n + 1) * qb, nb - 1)
    no_prev = lambda n: (n == 0).astype(jnp.int32)
    no_next = lambda n: 2 * (n == steps - 1).astype(jnp.int32)
    first = lambda bi, n: (no_prev(n) + (no_next(n) if qb == 1 else 0), 0, 0, 0)
    mid = lambda bi, n: (0, 0, 0, 0)
    last = lambda bi, n: (no_next(n), 0, 0, 0)
    k_halo = lambda f: pl.BlockSpec((1, BLOCK, KV_DIM), lambda bi, n: (bi, f(n), 0))
    vt_halo = lambda f: pl.BlockSpec((1, KV_DIM, BLOCK), lambda bi, n: (bi, 0, f(n)))
    bias_spec = lambda imap: pl.BlockSpec((1, ATTN_KV_HEADS, 3 * BLOCK, ATTN_GROUP * BLOCK), imap)
    return pl.pallas_call(
        _attn_kernel,
        out_shape=jax.ShapeDtypeStruct((b, l, ATTN_WIDTH), BF16),
        grid=(b, steps),
        in_specs=[
            pl.BlockSpec((1, ATTN_WIDTH, qb * BLOCK), lambda bi, n: (bi, 0, n)),
            k_halo(before), pl.BlockSpec((1, qb * BLOCK, KV_DIM), lambda bi, n: (bi, n, 0)), k_halo(after),
            vt_halo(before), pl.BlockSpec((1, KV_DIM, qb * BLOCK), lambda bi, n: (bi, 0, n)), vt_halo(after),
            bias_spec(first), bias_spec(mid), bias_spec(last),
            pl.BlockSpec((ATTN_KV_HEADS, 8, ATTN_GROUP * BLOCK), lambda bi, n: (0, 0, 0)),
        ],
        out_specs=pl.BlockSpec((1, qb * BLOCK, ATTN_WIDTH), lambda bi, n: (bi, n, 0)),
        compiler_params=pltpu.CompilerParams(
            dimension_semantics=("arbitrary", "arbitrary"), vmem_limit_bytes=_vmem_limit(40 << 20)),
        name="swa_attention",
    )(qt, k, k, k, vt, vt, vt, bias, bias, bias, sink)


def _attn_tables(attn_sink):
    slopes = 2.0 ** (-8.0 * jnp.arange(1, ATTN_HEADS + 1, dtype=F32) / ATTN_HEADS)
    kj = jnp.arange(3 * BLOCK)[:, None]
    qi = jnp.arange(BLOCK)[None, :]
    dist = jnp.abs(qi - kj + BLOCK)
    bias = jnp.where(dist <= WINDOW, -slopes[:, None, None] * dist.astype(F32)[None], NEG_BIG)
    bias = bias.reshape(ATTN_KV_HEADS, ATTN_GROUP, 3 * BLOCK, BLOCK).transpose(0, 2, 1, 3)
    bias = bias.reshape(ATTN_KV_HEADS, 3 * BLOCK, ATTN_GROUP * BLOCK)
    no_prev = (kj < BLOCK)[None]
    no_next = (kj >= 2 * BLOCK)[None]
    bias = jnp.stack([jnp.where(hidden, NEG_BIG, bias)
                      for hidden in (jnp.zeros_like(no_prev), no_prev, no_next, no_prev | no_next)])
    sink = jnp.broadcast_to(attn_sink.astype(F32).reshape(ATTN_KV_HEADS, 1, ATTN_GROUP, 1),
                            (ATTN_KV_HEADS, 8, ATTN_GROUP, BLOCK))
    return bias, sink.reshape(ATTN_KV_HEADS, 8, ATTN_GROUP * BLOCK)


def _dn_scan_kernel(qf_ref, kf_ref, vf_ref, qb_ref, kb_ref, vb_ref,
                    gcolf_ref, gctf_ref, gcolb_ref, gctb_ref,
                    of_ref, ob_ref, state_ref):
    @pl.when(pl.program_id(1) == 0)
    def _():
        state_ref[...] = jnp.zeros_like(state_ref)

    n = SCAN_BLOCK
    row = lax.broadcasted_iota(jnp.int32, (n, n), 0)
    col = lax.broadcasted_iota(jnp.int32, (n, n), 1)
    same_chunk = (row // CHUNK) == (col // CHUNK)
    eye = (row == col).astype(F32)
    off_diag = row != col
    directions = (
        (qf_ref, kf_ref, vf_ref, gcolf_ref, gctf_ref, of_ref, same_chunk & (row >= col)),
        (qb_ref, kb_ref, vb_ref, gcolb_ref, gctb_ref, ob_ref, same_chunk & (row <= col)),
    )
    probs = [(d, h, b) for b in range(SCAN_STEP_BLOCKS) for d in range(2) for h in range(DN_HEADS)]
    kk, qq, vv, beta, gcol, egc, p, y, attn = ({} for _ in range(9))
    for pr in probs:
        d, h, b = pr
        q_ref, k_ref, v_ref, gcol_ref, gct_ref, _, incl = directions[d]
        lane = d * DN_HEADS + h
        blk = slice(b * n, (b + 1) * n)
        kk[pr] = k_ref[0, h, blk]
        qq[pr] = q_ref[0, h, blk]
        vv[pr] = v_ref[0, h, blk]
        beta[pr] = gcol_ref[0, blk][:, lane:lane + 1]
        gcol[pr] = gcol_ref[0, blk][:, 2 * DN_HEADS + lane:2 * DN_HEADS + lane + 1]
        grow = gct_ref[0, b][lane:lane + 1, :]
        gram = lax.dot_general(jnp.concatenate([kk[pr], qq[pr]], axis=0), kk[pr],
                               (((1,), (1,)), ((), ())), preferred_element_type=F32)
        decay = jnp.exp(jnp.where(incl, gcol[pr] - grow, NEG_BIG))
        neg_n = -(beta[pr] * gram[:n] * jnp.where(off_diag, decay, 0.0))
        attn[pr] = (gram[n:] * decay).astype(BF16)
        p[pr] = eye + neg_n
        y[pr] = neg_n.astype(BF16)
    for pr in probs:
        y[pr] = _dot(y[pr], y[pr])
    for _ in range(4):
        for pr in probs:
            yb = y[pr].astype(BF16)
            r = _dot(yb, jnp.concatenate([p[pr].astype(BF16), yb], axis=1))
            p[pr] = p[pr] + r[:, :n]
            y[pr] = r[:, n:]
    for pr in probs:
        p[pr] = p[pr] + _dot(y[pr].astype(BF16), p[pr].astype(BF16))
    uwb, auw, q_eff = {}, {}, {}
    for pr in probs:
        egc[pr] = jnp.exp(gcol[pr])
        rhs = jnp.concatenate([vv[pr].astype(F32) * beta[pr],
                               kk[pr].astype(F32) * (beta[pr] * egc[pr])], axis=1)
        uwb[pr] = _dot(p[pr].astype(BF16), rhs.astype(BF16)).astype(BF16)
    for pr in probs:
        auw[pr] = _dot(attn[pr], uwb[pr])
        q_eff[pr] = qq[pr].astype(F32) * egc[pr] - auw[pr][:, n:]
    chunks_per_block = n // CHUNK
    steps = SCAN_STEP_BLOCKS * chunks_per_block

    def locate(d, step):
        pos = step if d == 0 else steps - 1 - step
        return divmod(pos, chunks_per_block)

    bm, g_last = {}, {}
    for step in range(steps):
        for d in range(2):
            for h in range(DN_HEADS):
                b, c = locate(d, step)
                pr = (d, h, b)
                rs = slice(c * CHUNK, (c + 1) * CHUNK)
                end = c * CHUNK + (CHUNK - 1 if d == 0 else 0)
                g_last[pr, c] = gcol[pr][end:end + 1]
                k_dec = kk[pr][rs].astype(F32) * jnp.exp(g_last[pr, c] - gcol[pr][rs])
                bm[pr, c] = lax.dot_general(k_dec.astype(BF16), uwb[pr][rs], (((0,), (0,)), ((), ())),
                                            preferred_element_type=F32)
    for step in range(steps):
        for d in range(2):
            o_ref = directions[d][5]
            for h in range(DN_HEADS):
                b, c = locate(d, step)
                pr = (d, h, b)
                rs = slice(c * CHUNK, (c + 1) * CHUNK)
                lane = d * DN_HEADS + h
                s_old = state_ref[lane]
                lhs = jnp.concatenate([bm[pr, c][:, n:], q_eff[pr][rs]], axis=0).astype(BF16)
                ms = _dot(lhs, s_old.astype(BF16))
                state_ref[lane] = jnp.exp(g_last[pr, c]) * s_old - ms[:n] + bm[pr, c][:, :n]
                o_ref[0, b * n + c * CHUNK:b * n + (c + 1) * CHUNK, h * DN_HEAD_DIM:(h + 1) * DN_HEAD_DIM] = (
                    ms[n:] + auw[pr][rs, :n]).astype(BF16)


def _dn_scan(q, k, v, gcol, gct):
    b, _, l, _ = q.shape
    span = SCAN_STEP_BLOCKS * SCAN_BLOCK
    nb = l // span
    fwd4 = lambda bi, i: (bi, 0, i, 0)
    bwd4 = lambda bi, i: (bi, 0, nb - 1 - i, 0)
    fwd3 = lambda bi, i: (bi, i, 0)
    bwd3 = lambda bi, i: (bi, nb - 1 - i, 0)
    fwdt = lambda bi, i: (bi, i, 0, 0)
    bwdt = lambda bi, i: (bi, nb - 1 - i, 0, 0)
    qkv_spec = lambda imap: pl.BlockSpec((1, DN_HEADS, span, DN_HEAD_DIM), imap)
    gate_spec = lambda imap: pl.BlockSpec((1, span, V7X_LANES), imap)
    gct_spec = lambda imap: pl.BlockSpec((1, SCAN_STEP_BLOCKS, 2 * DN_HEADS, SCAN_BLOCK), imap)
    out = jax.ShapeDtypeStruct((b, l, DN_WIDTH), BF16)
    return pl.pallas_call(
        _dn_scan_kernel,
        out_shape=[out, out],
        grid=(b, nb),
        in_specs=[qkv_spec(fwd4), qkv_spec(fwd4), qkv_spec(fwd4),
                  qkv_spec(bwd4), qkv_spec(bwd4), qkv_spec(bwd4),
                  gate_spec(fwd3), gct_spec(fwdt), gate_spec(bwd3), gct_spec(bwdt)],
        out_specs=[pl.BlockSpec((1, span, DN_WIDTH), fwd3),
                   pl.BlockSpec((1, span, DN_WIDTH), bwd3)],
        scratch_shapes=[pltpu.VMEM((2 * DN_HEADS, DN_HEAD_DIM, DN_HEAD_DIM), F32)],
        compiler_params=pltpu.CompilerParams(
            dimension_semantics=("arbitrary", "arbitrary"), vmem_limit_bytes=_vmem_limit(40 << 20)),
        name="dn_scan",
    )(q, k, v, q, k, v, gcol, gct, gcol, gct)


def _mix_ffn_kernel(alpha, oa_ref, of_ref, ob_ref, z_ref, x_ref, w_ref, ng_ref, g2_ref, b2_ref,
                    win_ref, wout_ref, g3_ref, b3_ref, o_ref, dn_ref, act_ref):
    for part in range(FFN_TILE // MIX_FFN_PART):
        rs = slice(part * MIX_FFN_PART, (part + 1) * MIX_FFN_PART)
        for h in range(DN_HEADS):
            sl = slice(h * DN_HEAD_DIM, (h + 1) * DN_HEAD_DIM)
            o = of_ref[rs, sl].astype(F32) + ob_ref[rs, sl].astype(F32)
            ms = jnp.mean(o * o, axis=-1, keepdims=True)
            gate = _silu(z_ref[rs, sl].astype(F32))
            dn_ref[rs, sl] = (o * lax.rsqrt(ms + RMS_EPS) * ng_ref[...] * gate).astype(BF16)
        y = _dot(oa_ref[rs, :], w_ref[:ATTN_WIDTH, :]) + _dot(dn_ref[rs, :], w_ref[ATTN_WIDTH:, :])
        x = _layer_norm(alpha * x_ref[rs, :] + y, g2_ref[...], b2_ref[...])
        y = _swiglu(x.astype(BF16), win_ref, wout_ref, act_ref, rs)
        o_ref[rs, :] = _layer_norm(alpha * x + 0.5 * y, g3_ref[...], b3_ref[...])


def _mix_ffn(oa, of, ob, z, x2, w_out, norm_gain, gain2, bias2, w_in, w_ffn_out, gain3, bias3, alpha):
    rows = x2.shape[0]
    row = lambda i: (i, 0)
    const = lambda i: (0, 0)
    resident = lambda shape: pl.BlockSpec(shape, const, pipeline_mode=pl.Buffered(1))
    return pl.pallas_call(
        functools.partial(_mix_ffn_kernel, alpha),
        out_shape=jax.ShapeDtypeStruct((rows, D_MODEL), F32),
        grid=(rows // FFN_TILE,),
        in_specs=[
            pl.BlockSpec((FFN_TILE, ATTN_WIDTH), row),
            pl.BlockSpec((FFN_TILE, DN_WIDTH), row),
            pl.BlockSpec((FFN_TILE, DN_WIDTH), row),
            pl.BlockSpec((FFN_TILE, DN_WIDTH), row),
            pl.BlockSpec((FFN_TILE, D_MODEL), row),
            resident((D_MODEL, D_MODEL)),
            pl.BlockSpec((1, DN_HEAD_DIM), const),
            pl.BlockSpec((1, D_MODEL), const),
            pl.BlockSpec((1, D_MODEL), const),
            resident((D_MODEL, 2 * D_FF)),
            resident((D_FF, D_MODEL)),
            pl.BlockSpec((1, D_MODEL), const),
            pl.BlockSpec((1, D_MODEL), const),
        ],
        out_specs=pl.BlockSpec((FFN_TILE, D_MODEL), row),
        scratch_shapes=[pltpu.VMEM((FFN_TILE, DN_WIDTH), BF16), pltpu.VMEM((FFN_TILE, D_FF), BF16)],
        compiler_params=pltpu.CompilerParams(
            dimension_semantics=("arbitrary",), vmem_limit_bytes=_vmem_limit(60 << 20)),
        name="mix_ffn",
    )(oa, of, ob, z, x2, w_out, norm_gain, gain2, bias2, w_in, w_ffn_out, gain3, bias3)


def _pack_mixer_weights(w_in):
    aq = w_in[:, :ATTN_WIDTH]
    ak = w_in[:, ATTN_WIDTH:ATTN_WIDTH + KV_DIM]
    av = w_in[:, ATTN_WIDTH + KV_DIM:ATTN_WIDTH + 2 * KV_DIM]
    dqkv = w_in[:, ATTN_WIDTH + 2 * KV_DIM:ATTN_WIDTH + 2 * KV_DIM + 3 * DN_WIDTH]
    z = w_in[:, ATTN_WIDTH + 2 * KV_DIM + 3 * DN_WIDTH:OFF_B]
    gates = w_in[:, OFF_B:]
    w_row = jnp.concatenate([ak, z, dqkv], axis=1).astype(BF16)
    w_t = jnp.concatenate([aq, av, gates], axis=1).T.astype(BF16)
    return w_row, w_t


def _gate_params(a_log, dt_bias):
    cols = jnp.stack([a_log.reshape(-1), dt_bias.reshape(-1)], axis=1).astype(F32)
    return jnp.pad(cols, ((GATE_ROWS - cols.shape[0], 0), (0, V7X_LANES - cols.shape[1])))


def kernel(x_prompt, x_sample, ffn1_w_in, ffn1_w_out, w_in, conv_w, attn_sink, dn_a_log, dn_dt_bias,
           dn_norm_gain, w_out, ffn2_w_in, ffn2_w_out, ln_gain, ln_bias):
    depth = w_in.shape[0]
    alpha = (2.0 * depth) ** 0.25
    tri = _chunk_triangles()
    layers = []
    for i in range(depth):
        bias, sink = _attn_tables(attn_sink[i])
        w_row, w_t = _pack_mixer_weights(w_in[i])
        layers.append(dict(
            ffn1_in=ffn1_w_in[i].astype(BF16), ffn1_out=ffn1_w_out[i].astype(BF16),
            ffn2_in=ffn2_w_in[i].astype(BF16), ffn2_out=ffn2_w_out[i].astype(BF16),
            w_row=w_row, w_t=w_t, w_out=w_out[i].astype(BF16),
            conv=jnp.pad(conv_w[i].astype(F32), ((0, 8 - CONV_WIDTH), (0, 0))),
            gate=_gate_params(dn_a_log[i], dn_dt_bias[i]),
            norm_gain=dn_norm_gain[i].astype(F32).reshape(1, DN_HEAD_DIM),
            bias=bias, sink=sink,
            ln_g=ln_gain[i].astype(F32).reshape(3, 1, D_MODEL),
            ln_b=ln_bias[i].astype(F32).reshape(3, 1, D_MODEL),
        ))

    def trunk(x):
        b, l, _ = x.shape
        x2 = x.reshape(b * l, D_MODEL)
        for p in layers:
            x2 = _ffn_ln(x2, p["ffn1_in"], p["ffn1_out"], p["ln_g"][0], p["ln_b"][0], alpha)
            qt, ak, vt, z, dq, dk, dv, gcol, gct = _mixer_in(
                x2.reshape(b, l, D_MODEL), p["w_row"], p["w_t"], p["conv"], p["gate"], tri)
            oa = _attention(qt, ak, vt, p["bias"], p["sink"])
            of, ob = _dn_scan(dq, dk, dv, gcol, gct)
            x2 = _mix_ffn(oa.reshape(b * l, ATTN_WIDTH), of.reshape(b * l, DN_WIDTH),
                          ob.reshape(b * l, DN_WIDTH), z.reshape(b * l, DN_WIDTH), x2, p["w_out"],
                          p["norm_gain"], p["ln_g"][1], p["ln_b"][1],
                          p["ffn2_in"], p["ffn2_out"], p["ln_g"][2], p["ln_b"][2], alpha)
        return x2.reshape(b, l, D_MODEL)

    return (trunk(x_prompt), trunk(x_sample))
```

```python
import functools

import jax
import jax.numpy as jnp
from jax import lax
from jax.experimental import pallas as pl
from jax.experimental.pallas import tpu as pltpu

F32 = jnp.float32
BF16 = jnp.bfloat16

D_MODEL = 1024
ATTN_WIDTH = 512
ATTN_HEAD_DIM = 64
ATTN_HEADS = 8
ATTN_KV_HEADS = 2
ATTN_GROUP = ATTN_HEADS // ATTN_KV_HEADS
KV_DIM = ATTN_KV_HEADS * ATTN_HEAD_DIM
WINDOW = 128
BLOCK = 128
DN_WIDTH = 512
DN_HEAD_DIM = 128
DN_HEADS = DN_WIDTH // DN_HEAD_DIM
CONV_WIDTH = 5
CONV_PAD = CONV_WIDTH // 2
CHUNK = 64
D_FF = 2816
LN_EPS = 1e-5
RMS_EPS = 1e-6
OFF_B = ATTN_WIDTH + 2 * KV_DIM + 4 * DN_WIDTH
OFF_A = OFF_B + 2 * DN_HEADS

V7X_LANES = 128
V7X_BF16_SUBLANES = 16
V7X_VMEM_BYTES = 64 * 1024 * 1024

FFN_TILE = 1024
FFN_PART = 256
FFN_CHUNK = 256
ATTN_QBLOCKS = 4
MIX_TILE = 512
MIX_HALO = V7X_BF16_SUBLANES
SCAN_BLOCK = 2 * CHUNK
SCAN_STEP_BLOCKS = 2
GATE_ROWS = 4 * DN_HEADS
NEG_BIG = -1e30


def _vmem_limit(nbytes):
    return int(min(nbytes, V7X_VMEM_BYTES - 4 * 1024 * 1024))


def _layer_norm(r, gain, bias):
    mu = jnp.mean(r, axis=-1, keepdims=True)
    c = r - mu
    var = jnp.mean(c * c, axis=-1, keepdims=True)
    return c * lax.rsqrt(var + LN_EPS) * gain + bias


def _silu(t):
    return t * jax.nn.sigmoid(t)


def _dot(a, b):
    return jnp.dot(a, b, preferred_element_type=F32)


def _swiglu(xb, win_ref, wout_ref, act_ref, rs):
    for j in range(D_FF // FFN_CHUNK):
        lo = j * FFN_CHUNK
        gate = _dot(xb, win_ref[:, lo:lo + FFN_CHUNK])
        up = _dot(xb, win_ref[:, D_FF + lo:D_FF + lo + FFN_CHUNK])
        act_ref[rs, lo:lo + FFN_CHUNK] = (_silu(gate) * up).astype(BF16)
    return _dot(act_ref[rs, :], wout_ref[...])


def _ffn_ln_kernel(alpha, x_ref, win_ref, wout_ref, g_ref, b_ref, o_ref, act_ref):
    for part in range(FFN_TILE // FFN_PART):
        rs = slice(part * FFN_PART, (part + 1) * FFN_PART)
        x = x_ref[rs, :]
        y = _swiglu(x.astype(BF16), win_ref, wout_ref, act_ref, rs)
        o_ref[rs, :] = _layer_norm(alpha * x + 0.5 * y, g_ref[...], b_ref[...])


def _ffn_ln(x2, w_in, w_out, gain, bias, alpha):
    rows = x2.shape[0]
    const = lambda i: (0, 0)
    return pl.pallas_call(
        functools.partial(_ffn_ln_kernel, alpha),
        out_shape=jax.ShapeDtypeStruct((rows, D_MODEL), F32),
        grid=(rows // FFN_TILE,),
        in_specs=[
            pl.BlockSpec((FFN_TILE, D_MODEL), lambda i: (i, 0)),
            pl.BlockSpec((D_MODEL, 2 * D_FF), const, pipeline_mode=pl.Buffered(1)),
            pl.BlockSpec((D_FF, D_MODEL), const, pipeline_mode=pl.Buffered(1)),
            pl.BlockSpec((1, D_MODEL), const),
            pl.BlockSpec((1, D_MODEL), const),
        ],
        out_specs=pl.BlockSpec((FFN_TILE, D_MODEL), lambda i: (i, 0)),
        scratch_shapes=[pltpu.VMEM((FFN_TILE, D_FF), BF16)],
        compiler_params=pltpu.CompilerParams(
            dimension_semantics=("arbitrary",), vmem_limit_bytes=_vmem_limit(56 << 20)),
        name="ffn_ln",
    )(x2, w_in, w_out, gain, bias)


_W_ROW_K = 0
_W_ROW_Z = KV_DIM
_W_ROW_DQKV = KV_DIM + DN_WIDTH
_W_ROW_COLS = KV_DIM + DN_WIDTH + 3 * DN_WIDTH
_W_T_Q = 0
_W_T_V = ATTN_WIDTH
_W_T_GATE = ATTN_WIDTH + KV_DIM
_W_T_ROWS = _W_T_GATE + GATE_ROWS


def _mixer_in_kernel(x_ref, xp_ref, xn_ref, wrow_ref, wt_ref, conv_ref, gp_ref, tri_ref,
                     qt_ref, k_ref, vt_ref, z_ref, dq_ref, dk_ref, dv_ref, gcol_ref, gct_ref, proj_ref):
    i = pl.program_id(1)
    last = pl.num_programs(1) - 1
    tile, halo = MIX_TILE, MIX_HALO
    prev = jnp.where(i > 0, xp_ref[0], 0.0)
    nxt = jnp.where(i < last, xn_ref[0], 0.0)
    xe = jnp.concatenate([prev, x_ref[0], nxt], axis=0).astype(BF16)
    xb = xe[halo:halo + tile]

    tproj = lax.dot_general(wt_ref[...], xb, (((1,), (1,)), ((), ())), preferred_element_type=F32)
    qt_ref[0] = tproj[_W_T_Q:_W_T_V].astype(BF16)
    vt_ref[0] = tproj[_W_T_V:_W_T_GATE].astype(BF16)
    gate_logits = tproj[_W_T_GATE:_W_T_ROWS]

    k_ref[0] = _dot(xb, wrow_ref[:, _W_ROW_K:_W_ROW_Z]).astype(BF16)
    z_ref[0] = _dot(xb, wrow_ref[:, _W_ROW_Z:_W_ROW_DQKV]).astype(BF16)

    for kind, o_ref in enumerate((dq_ref, dk_ref, dv_ref)):
        lo = _W_ROW_DQKV + kind * DN_WIDTH
        proj = _dot(xe, wrow_ref[:, lo:lo + DN_WIDTH])
        for h in range(DN_HEADS):
            proj_ref[h] = proj[:, h * DN_HEAD_DIM:(h + 1) * DN_HEAD_DIM]
        for h in range(DN_HEADS):
            col = kind * DN_WIDTH + h * DN_HEAD_DIM
            y = jnp.zeros((tile, DN_HEAD_DIM), F32)
            for t in range(CONV_WIDTH):
                first = halo + t - CONV_PAD
                xs = proj_ref[h, pl.ds(first, tile), :]
                y = y + conv_ref[t:t + 1, col:col + DN_HEAD_DIM] * xs
            s = _silu(y)
            if kind < 2:
                s = s * lax.rsqrt(jnp.sum(s * s, axis=-1, keepdims=True) + RMS_EPS)
            if kind == 0:
                s = s * (DN_HEAD_DIM ** -0.5)
            o_ref[0, h] = s.astype(BF16)

    row = lax.broadcasted_iota(jnp.int32, (GATE_ROWS, SCAN_BLOCK), 0)
    beta = jax.nn.sigmoid(gate_logits)
    t = gate_logits + gp_ref[:, 1:2]
    softplus = jnp.maximum(t, 0.0) + jnp.log1p(jnp.exp(-jnp.abs(t)))
    g = -jnp.exp(gp_ref[:, 0:1]) * softplus
    g_hi = g.astype(BF16)
    r1 = g - g_hi.astype(F32)
    g_mid = r1.astype(BF16)
    g_lo = (r1 - g_mid.astype(F32)).astype(BF16)
    is_fwd = row < 2 * DN_HEADS + DN_HEADS
    pad_rows = jnp.zeros((SCAN_BLOCK - GATE_ROWS, SCAN_BLOCK), F32)
    for r in range(tile // SCAN_BLOCK):
        ls = slice(r * SCAN_BLOCK, (r + 1) * SCAN_BLOCK)
        parts = _dot(jnp.concatenate([g_hi[:, ls], g_mid[:, ls], g_lo[:, ls]], axis=0), tri_ref[...])
        sums = parts[:GATE_ROWS] + parts[GATE_ROWS:2 * GATE_ROWS] + parts[2 * GATE_ROWS:]
        gc = jnp.where(is_fwd, sums[:, :SCAN_BLOCK], sums[:, SCAN_BLOCK:])
        gct_ref[0, r] = gc[2 * DN_HEADS:]
        both = jnp.where(row < 2 * DN_HEADS, beta[:, ls], gc)
        gcol_ref[0, ls, :] = jnp.concatenate([both, pad_rows], axis=0).T


def _mixer_in(x, w_row, w_t, conv_w, gate_params, tri):
    b, l, _ = x.shape
    per = MIX_TILE // MIX_HALO
    nh = l // MIX_HALO
    cur = lambda bi, i: (bi, i, 0)
    const2 = lambda bi, i: (0, 0)
    head_major = jax.ShapeDtypeStruct((b, DN_HEADS, l, DN_HEAD_DIM), BF16)
    head_spec = pl.BlockSpec((1, DN_HEADS, MIX_TILE, DN_HEAD_DIM), lambda bi, i: (bi, 0, i, 0))
    return pl.pallas_call(
        _mixer_in_kernel,
        out_shape=[
            jax.ShapeDtypeStruct((b, ATTN_WIDTH, l), BF16),
            jax.ShapeDtypeStruct((b, l, KV_DIM), BF16),
            jax.ShapeDtypeStruct((b, KV_DIM, l), BF16),
            jax.ShapeDtypeStruct((b, l, DN_WIDTH), BF16),
            head_major, head_major, head_major,
            jax.ShapeDtypeStruct((b, l, V7X_LANES), F32),
            jax.ShapeDtypeStruct((b, l // SCAN_BLOCK, 2 * DN_HEADS, SCAN_BLOCK), F32),
        ],
        grid=(b, l // MIX_TILE),
        in_specs=[
            pl.BlockSpec((1, MIX_TILE, D_MODEL), cur),
            pl.BlockSpec((1, MIX_HALO, D_MODEL), lambda bi, i: (bi, jnp.maximum(i * per - 1, 0), 0)),
            pl.BlockSpec((1, MIX_HALO, D_MODEL), lambda bi, i: (bi, jnp.minimum((i + 1) * per, nh - 1), 0)),
            pl.BlockSpec((D_MODEL, _W_ROW_COLS), const2, pipeline_mode=pl.Buffered(1)),
            pl.BlockSpec((_W_T_ROWS, D_MODEL), const2, pipeline_mode=pl.Buffered(1)),
            pl.BlockSpec((8, 3 * DN_WIDTH), const2),
            pl.BlockSpec((GATE_ROWS, V7X_LANES), const2),
            pl.BlockSpec((SCAN_BLOCK, 2 * SCAN_BLOCK), const2),
        ],
        out_specs=[
            pl.BlockSpec((1, ATTN_WIDTH, MIX_TILE), lambda bi, i: (bi, 0, i)),
            pl.BlockSpec((1, MIX_TILE, KV_DIM), cur),
            pl.BlockSpec((1, KV_DIM, MIX_TILE), lambda bi, i: (bi, 0, i)),
            pl.BlockSpec((1, MIX_TILE, DN_WIDTH), cur),
            head_spec, head_spec, head_spec,
            pl.BlockSpec((1, MIX_TILE, V7X_LANES), cur),
            pl.BlockSpec((1, MIX_TILE // SCAN_BLOCK, 2 * DN_HEADS, SCAN_BLOCK), lambda bi, i: (bi, i, 0, 0)),
        ],
        scratch_shapes=[pltpu.VMEM((DN_HEADS, MIX_TILE + 2 * MIX_HALO, DN_HEAD_DIM), F32)],
        compiler_params=pltpu.CompilerParams(
            dimension_semantics=("arbitrary", "arbitrary"), vmem_limit_bytes=_vmem_limit(48 << 20)),
        name="mixer_in",
    )(x, x, x, w_row, w_t, conv_w, gate_params, tri)


def _chunk_triangles():
    t = jnp.arange(SCAN_BLOCK)
    same = (t[:, None] // CHUNK) == (t[None, :] // CHUNK)
    prefix = same & (t[:, None] <= t[None, :])
    suffix = same & (t[:, None] >= t[None, :])
    return jnp.concatenate([prefix, suffix], axis=1).astype(BF16)


def _attn_kernel(qt_ref, kp_ref, kc_ref, kn_ref, vtp_ref, vtc_ref, vtn_ref,
                 bias_first_ref, bias_mid_ref, bias_last_ref, sink_ref, o_ref):
    qt = qt_ref[0] * jnp.asarray(ATTN_HEAD_DIM ** -0.5, BF16)
    kcat = jnp.concatenate([kp_ref[0], kc_ref[0], kn_ref[0]], axis=0)
    vtcat = jnp.concatenate([vtp_ref[0], vtc_ref[0], vtn_ref[0]], axis=1)
    no_head = jnp.zeros((ATTN_HEAD_DIM, BLOCK), BF16)
    items = [(j, h) for j in range(ATTN_QBLOCKS) for h in range(ATTN_KV_HEADS)]
    scores, outs = {}, {}
    for j, h in items:
        cols = []
        for g in range(ATTN_GROUP):
            lo = (h * ATTN_GROUP + g) * ATTN_HEAD_DIM
            qhg = qt[lo:lo + ATTN_HEAD_DIM, j * BLOCK:(j + 1) * BLOCK]
            cols.append(jnp.concatenate([qhg, no_head] if h == 0 else [no_head, qhg], axis=0))
        scores[j, h] = _dot(kcat[j * BLOCK:(j + 3) * BLOCK], jnp.concatenate(cols, axis=1))
    for j, h in items:
        bias_ref = bias_first_ref if j == 0 else (bias_last_ref if j == ATTN_QBLOCKS - 1 else bias_mid_ref)
        logits = scores[j, h] + bias_ref[0, h]
        sink = sink_ref[h][0:1, :]
        m = jnp.maximum(jnp.max(logits, axis=0, keepdims=True), sink)
        p = jnp.exp(logits - m)
        denom = jnp.sum(p, axis=0, keepdims=True) + jnp.exp(sink - m)
        vt_h = vtcat[h * ATTN_HEAD_DIM:(h + 1) * ATTN_HEAD_DIM, j * BLOCK:(j + 3) * BLOCK]
        pv = _dot(vt_h, p.astype(BF16)) / denom
        outs[j, h] = [pv[:, g * BLOCK:(g + 1) * BLOCK] for g in range(ATTN_GROUP)]
    for j in range(ATTN_QBLOCKS):
        pieces = [piece for h in range(ATTN_KV_HEADS) for piece in outs[j, h]]
        o_ref[0, j * BLOCK:(j + 1) * BLOCK, :] = jnp.concatenate(pieces, axis=0).T.astype(BF16)


def _attention(qt, k, vt, bias, sink):
    b, _, l = qt.shape
    qb = ATTN_QBLOCKS
    nb = l // BLOCK
    steps = nb // qb
    before = lambda n: jnp.maximum(n * qb - 1, 0)
    after = lambda n: jnp.minimum((n + 1) * qb, nb - 1)
    no_prev = lambda n: (n == 0).astype(jnp.int32)
    no_next = lambda n: 2 * (n == steps - 1).astype(jnp.int32)
    first = lambda bi, n: (no_prev(n) + (no_next(n) if qb == 1 else 0), 0, 0, 0)
    mid = lambda bi, n: (0, 0, 0, 0)
    last = lambda bi, n: (no_next(n), 0, 0, 0)
    k_halo = lambda f: pl.BlockSpec((1, BLOCK, KV_DIM), lambda bi, n: (bi, f(n), 0))
    vt_halo = lambda f: pl.BlockSpec((1, KV_DIM, BLOCK), lambda bi, n: (bi, 0, f(n)))
    bias_spec = lambda imap: pl.BlockSpec((1, ATTN_KV_HEADS, 3 * BLOCK, ATTN_GROUP * BLOCK), imap)
    return pl.pallas_call(
        _attn_kernel,
        out_shape=jax.ShapeDtypeStruct((b, l, ATTN_WIDTH), BF16),
        grid=(b, steps),
        in_specs=[
            pl.BlockSpec((1, ATTN_WIDTH, qb * BLOCK), lambda bi, n: (bi, 0, n)),
            k_halo(before), pl.BlockSpec((1, qb * BLOCK, KV_DIM), lambda bi, n: (bi, n, 0)), k_halo(after),
            vt_halo(before), pl.BlockSpec((1, KV_DIM, qb * BLOCK), lambda bi, n: (bi, 0, n)), vt_halo(after),
            bias_spec(first), bias_spec(mid), bias_spec(last),
            pl.BlockSpec((ATTN_KV_HEADS, 8, ATTN_GROUP * BLOCK), lambda bi, n: (0, 0, 0)),
        ],
        out_specs=pl.BlockSpec((1, qb * BLOCK, ATTN_WIDTH), lambda bi, n: (bi, n, 0)),
        compiler_params=pltpu.CompilerParams(
            dimension_semantics=("arbitrary", "arbitrary"), vmem_limit_bytes=_vmem_limit(40 << 20)),
        name="swa_attention",
    )(qt, k, k, k, vt, vt, vt, bias, bias, bias, sink)


def _attn_tables(attn_sink):
    slopes = 2.0 ** (-8.0 * jnp.arange(1, ATTN_HEADS + 1, dtype=F32) / ATTN_HEADS)
    kj = jnp.arange(3 * BLOCK)[:, None]
    qi = jnp.arange(BLOCK)[None, :]
    dist = jnp.abs(qi - kj + BLOCK)
    bias = jnp.where(dist <= WINDOW, -slopes[:, None, None] * dist.astype(F32)[None], NEG_BIG)
    bias = bias.reshape(ATTN_KV_HEADS, ATTN_GROUP, 3 * BLOCK, BLOCK).transpose(0, 2, 1, 3)
    bias = bias.reshape(ATTN_KV_HEADS, 3 * BLOCK, ATTN_GROUP * BLOCK)
    no_prev = (kj < BLOCK)[None]
    no_next = (kj >= 2 * BLOCK)[None]
    bias = jnp.stack([jnp.where(hidden, NEG_BIG, bias)
                      for hidden in (jnp.zeros_like(no_prev), no_prev, no_next, no_prev | no_next)])
    sink = jnp.broadcast_to(attn_sink.astype(F32).reshape(ATTN_KV_HEADS, 1, ATTN_GROUP, 1),
                            (ATTN_KV_HEADS, 8, ATTN_GROUP, BLOCK))
    return bias, sink.reshape(ATTN_KV_HEADS, 8, ATTN_GROUP * BLOCK)


def _dn_scan_kernel(qf_ref, kf_ref, vf_ref, qb_ref, kb_ref, vb_ref,
                    gcolf_ref, gctf_ref, gcolb_ref, gctb_ref,
                    of_ref, ob_ref, state_ref):
    @pl.when(pl.program_id(1) == 0)
    def _():
        state_ref[...] = jnp.zeros_like(state_ref)

    n = SCAN_BLOCK
    row = lax.broadcasted_iota(jnp.int32, (n, n), 0)
    col = lax.broadcasted_iota(jnp.int32, (n, n), 1)
    same_chunk = (row // CHUNK) == (col // CHUNK)
    eye = (row == col).astype(F32)
    off_diag = row != col
    directions = (
        (qf_ref, kf_ref, vf_ref, gcolf_ref, gctf_ref, of_ref, same_chunk & (row >= col)),
        (qb_ref, kb_ref, vb_ref, gcolb_ref, gctb_ref, ob_ref, same_chunk & (row <= col)),
    )
    probs = [(d, h, b) for b in range(SCAN_STEP_BLOCKS) for d in range(2) for h in range(DN_HEADS)]
    kk, qq, vv, beta, gcol, egc, p, y, attn = ({} for _ in range(9))
    for pr in probs:
        d, h, b = pr
        q_ref, k_ref, v_ref, gcol_ref, gct_ref, _, incl = directions[d]
        lane = d * DN_HEADS + h
        blk = slice(b * n, (b + 1) * n)
        kk[pr] = k_ref[0, h, blk]
        qq[pr] = q_ref[0, h, blk]
        vv[pr] = v_ref[0, h, blk]
        beta[pr] = gcol_ref[0, blk][:, lane:lane + 1]
        gcol[pr] = gcol_ref[0, blk][:, 2 * DN_HEADS + lane:2 * DN_HEADS + lane + 1]
        grow = gct_ref[0, b][lane:lane + 1, :]
        gram = lax.dot_general(jnp.concatenate([kk[pr], qq[pr]], axis=0), kk[pr],
                               (((1,), (1,)), ((), ())), preferred_element_type=F32)
        decay = jnp.exp(jnp.where(incl, gcol[pr] - grow, NEG_BIG))
        neg_n = -(beta[pr] * gram[:n] * jnp.where(off_diag, decay, 0.0))
        attn[pr] = (gram[n:] * decay).astype(BF16)
        p[pr] = eye + neg_n
        y[pr] = neg_n.astype(BF16)
    for pr in probs:
        y[pr] = _dot(y[pr], y[pr])
    for _ in range(4):
        for pr in probs:
            yb = y[pr].astype(BF16)
            r = _dot(yb, jnp.concatenate([p[pr].astype(BF16), yb], axis=1))
            p[pr] = p[pr] + r[:, :n]
            y[pr] = r[:, n:]
    for pr in probs:
        p[pr] = p[pr] + _dot(y[pr].astype(BF16), p[pr].astype(BF16))
    uwb, auw, q_eff = {}, {}, {}
    for pr in probs:
        egc[pr] = jnp.exp(gcol[pr])
        rhs = jnp.concatenate([vv[pr].astype(F32) * beta[pr],
                               kk[pr].astype(F32) * (beta[pr] * egc[pr])], axis=1)
        uwb[pr] = _dot(p[pr].astype(BF16), rhs.astype(BF16)).astype(BF16)
    for pr in probs:
        auw[pr] = _dot(attn[pr], uwb[pr])
        q_eff[pr] = qq[pr].astype(F32) * egc[pr] - auw[pr][:, n:]
    chunks_per_block = n // CHUNK
    steps = SCAN_STEP_BLOCKS * chunks_per_block

    def locate(d, step):
        pos = step if d == 0 else steps - 1 - step
        return divmod(pos, chunks_per_block)

    bm, g_last = {}, {}
    for step in range(steps):
        for d in range(2):
            for h in range(DN_HEADS):
                b, c = locate(d, step)
                pr = (d, h, b)
                rs = slice(c * CHUNK, (c + 1) * CHUNK)
                end = c * CHUNK + (CHUNK - 1 if d == 0 else 0)
                g_last[pr, c] = gcol[pr][end:end + 1]
                k_dec = kk[pr][rs].astype(F32) * jnp.exp(g_last[pr, c] - gcol[pr][rs])
                bm[pr, c] = lax.dot_general(k_dec.astype(BF16), uwb[pr][rs], (((0,), (0,)), ((), ())),
                                            preferred_element_type=F32)
    for step in range(steps):
        for d in range(2):
            o_ref = directions[d][5]
            for h in range(DN_HEADS):
                b, c = locate(d, step)
                pr = (d, h, b)
                rs = slice(c * CHUNK, (c + 1) * CHUNK)
                lane = d * DN_HEADS + h
                s_old = state_ref[lane]
                lhs = jnp.concatenate([bm[pr, c][:, n:], q_eff[pr][rs]], axis=0).astype(BF16)
                ms = _dot(lhs, s_old.astype(BF16))
                state_ref[lane] = jnp.exp(g_last[pr, c]) * s_old - ms[:n] + bm[pr, c][:, :n]
                o_ref[0, b * n + c * CHUNK:b * n + (c + 1) * CHUNK, h * DN_HEAD_DIM:(h + 1) * DN_HEAD_DIM] = (
                    ms[n:] + auw[pr][rs, :n]).astype(BF16)


def _dn_scan(q, k, v, gcol, gct):
    b, _, l, _ = q.shape
    span = SCAN_STEP_BLOCKS * SCAN_BLOCK
    nb = l // span
    fwd4 = lambda bi, i: (bi, 0, i, 0)
    bwd4 = lambda bi, i: (bi, 0, nb - 1 - i, 0)
    fwd3 = lambda bi, i: (bi, i, 0)
    bwd3 = lambda bi, i: (bi, nb - 1 - i, 0)
    fwdt = lambda bi, i: (bi, i, 0, 0)
    bwdt = lambda bi, i: (bi, nb - 1 - i, 0, 0)
    qkv_spec = lambda imap: pl.BlockSpec((1, DN_HEADS, span, DN_HEAD_DIM), imap)
    gate_spec = lambda imap: pl.BlockSpec((1, span, V7X_LANES), imap)
    gct_spec = lambda imap: pl.BlockSpec((1, SCAN_STEP_BLOCKS, 2 * DN_HEADS, SCAN_BLOCK), imap)
    out = jax.ShapeDtypeStruct((b, l, DN_WIDTH), BF16)
    return pl.pallas_call(
        _dn_scan_kernel,
        out_shape=[out, out],
        grid=(b, nb),
        in_specs=[qkv_spec(fwd4), qkv_spec(fwd4), qkv_spec(fwd4),
                  qkv_spec(bwd4), qkv_spec(bwd4), qkv_spec(bwd4),
                  gate_spec(fwd3), gct_spec(fwdt), gate_spec(bwd3), gct_spec(bwdt)],
        out_specs=[pl.BlockSpec((1, span, DN_WIDTH), fwd3),
                   pl.BlockSpec((1, span, DN_WIDTH), bwd3)],
        scratch_shapes=[pltpu.VMEM((2 * DN_HEADS, DN_HEAD_DIM, DN_HEAD_DIM), F32)],
        compiler_params=pltpu.CompilerParams(
            dimension_semantics=("arbitrary", "arbitrary"), vmem_limit_bytes=_vmem_limit(40 << 20)),
        name="dn_scan",
    )(q, k, v, q, k, v, gcol, gct, gcol, gct)


def _mix_ffn_kernel(alpha, oa_ref, of_ref, ob_ref, z_ref, x_ref, w_ref, ng_ref, g2_ref, b2_ref,
                    win_ref, wout_ref, g3_ref, b3_ref, o_ref, dn_ref, act_ref):
    parts = [slice(i * FFN_PART, (i + 1) * FFN_PART) for i in range(FFN_TILE // FFN_PART)]
    mixed = []
    for rs in parts:
        for h in range(DN_HEADS):
            sl = slice(h * DN_HEAD_DIM, (h + 1) * DN_HEAD_DIM)
            o = of_ref[rs, sl].astype(F32) + ob_ref[rs, sl].astype(F32)
            ms = jnp.mean(o * o, axis=-1, keepdims=True)
            gate = _silu(z_ref[rs, sl].astype(F32))
            dn_ref[rs, sl] = (o * lax.rsqrt(ms + RMS_EPS) * ng_ref[...] * gate).astype(BF16)
        y = _dot(oa_ref[rs, :], w_ref[:ATTN_WIDTH, :]) + _dot(dn_ref[rs, :], w_ref[ATTN_WIDTH:, :])
        mixed.append(_layer_norm(alpha * x_ref[rs, :] + y, g2_ref[...], b2_ref[...]))
    for rs, x in zip(parts, mixed):
        y = _swiglu(x.astype(BF16), win_ref, wout_ref, act_ref, rs)
        o_ref[rs, :] = _layer_norm(alpha * x + 0.5 * y, g3_ref[...], b3_ref[...])


def _mix_ffn(oa, of, ob, z, x2, w_out, norm_gain, gain2, bias2, w_in, w_ffn_out, gain3, bias3, alpha):
    rows = x2.shape[0]
    row = lambda i: (i, 0)
    const = lambda i: (0, 0)
    resident = lambda shape: pl.BlockSpec(shape, const, pipeline_mode=pl.Buffered(1))
    return pl.pallas_call(
        functools.partial(_mix_ffn_kernel, alpha),
        out_shape=jax.ShapeDtypeStruct((rows, D_MODEL), F32),
        grid=(rows // FFN_TILE,),
        in_specs=[
            pl.BlockSpec((FFN_TILE, ATTN_WIDTH), row),
            pl.BlockSpec((FFN_TILE, DN_WIDTH), row),
            pl.BlockSpec((FFN_TILE, DN_WIDTH), row),
            pl.BlockSpec((FFN_TILE, DN_WIDTH), row),
            pl.BlockSpec((FFN_TILE, D_MODEL), row),
            resident((D_MODEL, D_MODEL)),
            pl.BlockSpec((1, DN_HEAD_DIM), const),
            pl.BlockSpec((1, D_MODEL), const),
            pl.BlockSpec((1, D_MODEL), const),
            resident((D_MODEL, 2 * D_FF)),
            resident((D_FF, D_MODEL)),
            pl.BlockSpec((1, D_MODEL), const),
            pl.BlockSpec((1, D_MODEL), const),
        ],
        out_specs=pl.BlockSpec((FFN_TILE, D_MODEL), row),
        scratch_shapes=[pltpu.VMEM((FFN_TILE, DN_WIDTH), BF16), pltpu.VMEM((FFN_TILE, D_FF), BF16)],
        compiler_params=pltpu.CompilerParams(
            dimension_semantics=("arbitrary",), vmem_limit_bytes=_vmem_limit(60 << 20)),
        name="mix_ffn",
    )(oa, of, ob, z, x2, w_out, norm_gain, gain2, bias2, w_in, w_ffn_out, gain3, bias3)


def _pack_mixer_weights(w_in):
    aq = w_in[:, :ATTN_WIDTH]
    ak = w_in[:, ATTN_WIDTH:ATTN_WIDTH + KV_DIM]
    av = w_in[:, ATTN_WIDTH + KV_DIM:ATTN_WIDTH + 2 * KV_DIM]
    dqkv = w_in[:, ATTN_WIDTH + 2 * KV_DIM:ATTN_WIDTH + 2 * KV_DIM + 3 * DN_WIDTH]
    z = w_in[:, ATTN_WIDTH + 2 * KV_DIM + 3 * DN_WIDTH:OFF_B]
    gates = w_in[:, OFF_B:]
    w_row = jnp.concatenate([ak, z, dqkv], axis=1).astype(BF16)
    w_t = jnp.concatenate([aq, av, gates], axis=1).T.astype(BF16)
    return w_row, w_t


def _gate_params(a_log, dt_bias):
    cols = jnp.stack([a_log.reshape(-1), dt_bias.reshape(-1)], axis=1).astype(F32)
    return jnp.pad(cols, ((GATE_ROWS - cols.shape[0], 0), (0, V7X_LANES - cols.shape[1])))


def kernel(x_prompt, x_sample, ffn1_w_in, ffn1_w_out, w_in, conv_w, attn_sink, dn_a_log, dn_dt_bias,
           dn_norm_gain, w_out, ffn2_w_in, ffn2_w_out, ln_gain, ln_bias):
    depth = w_in.shape[0]
    alpha = (2.0 * depth) ** 0.25
    tri = _chunk_triangles()
    layers = []
    for i in range(depth):
        bias, sink = _attn_tables(attn_sink[i])
        w_row, w_t = _pack_mixer_weights(w_in[i])
        layers.append(dict(
            ffn1_in=ffn1_w_in[i].astype(BF16), ffn1_out=ffn1_w_out[i].astype(BF16),
            ffn2_in=ffn2_w_in[i].astype(BF16), ffn2_out=ffn2_w_out[i].astype(BF16),
            w_row=w_row, w_t=w_t, w_out=w_out[i].astype(BF16),
            conv=jnp.pad(conv_w[i].astype(F32), ((0, 8 - CONV_WIDTH), (0, 0))),
            gate=_gate_params(dn_a_log[i], dn_dt_bias[i]),
            norm_gain=dn_norm_gain[i].astype(F32).reshape(1, DN_HEAD_DIM),
            bias=bias, sink=sink,
            ln_g=ln_gain[i].astype(F32).reshape(3, 1, D_MODEL),
            ln_b=ln_bias[i].astype(F32).reshape(3, 1, D_MODEL),
        ))

    def trunk(x):
        b, l, _ = x.shape
        x2 = x.reshape(b * l, D_MODEL)
        for p in layers:
            x2 = _ffn_ln(x2, p["ffn1_in"], p["ffn1_out"], p["ln_g"][0], p["ln_b"][0], alpha)
            qt, ak, vt, z, dq, dk, dv, gcol, gct = _mixer_in(
                x2.reshape(b, l, D_MODEL), p["w_row"], p["w_t"], p["conv"], p["gate"], tri)
            oa = _attention(qt, ak, vt, p["bias"], p["sink"])
            of, ob = _dn_scan(dq, dk, dv, gcol, gct)
            x2 = _mix_ffn(oa.reshape(b * l, ATTN_WIDTH), of.reshape(b * l, DN_WIDTH),
                          ob.reshape(b * l, DN_WIDTH), z.reshape(b * l, DN_WIDTH), x2, p["w_out"],
                          p["norm_gain"], p["ln_g"][1], p["ln_b"][1],
                          p["ffn2_in"], p["ffn2_out"], p["ln_g"][2], p["ln_b"][2], alpha)
        return x2.reshape(b, l, D_MODEL)

    return (trunk(x_prompt), trunk(x_sample))
```

```python
import functools

import jax
import jax.numpy as jnp
from jax import lax
from jax.experimental import pallas as pl
from jax.experimental.pallas import tpu as pltpu

F32 = jnp.float32
BF16 = jnp.bfloat16

D_MODEL = 1024
ATTN_WIDTH = 512
ATTN_HEAD_DIM = 64
ATTN_HEADS = 8
ATTN_KV_HEADS = 2
ATTN_GROUP = ATTN_HEADS // ATTN_KV_HEADS
KV_DIM = ATTN_KV_HEADS * ATTN_HEAD_DIM
WINDOW = 128
BLOCK = 128
DN_WIDTH = 512
DN_HEAD_DIM = 128
DN_HEADS = DN_WIDTH // DN_HEAD_DIM
CONV_WIDTH = 5
CONV_PAD = CONV_WIDTH // 2
CHUNK = 64
D_FF = 2816
LN_EPS = 1e-5
RMS_EPS = 1e-6
OFF_B = ATTN_WIDTH + 2 * KV_DIM + 4 * DN_WIDTH
OFF_A = OFF_B + 2 * DN_HEADS

V7X_LANES = 128
V7X_BF16_SUBLANES = 16
V7X_VMEM_BYTES = 64 * 1024 * 1024

FFN_TILE = 1024
FFN_PART = 256
FFN_CHUNK = 256
ATTN_QBLOCKS = 8
MIX_TILE = 1024
MIX_HALO = V7X_BF16_SUBLANES
SCAN_BLOCK = 2 * CHUNK
SCAN_STEP_BLOCKS = 4
GATE_ROWS = 4 * DN_HEADS
NEG_BIG = -1e30


def _vmem_limit(nbytes):
    return int(min(nbytes, V7X_VMEM_BYTES - 4 * 1024 * 1024))


def _layer_norm(r, gain, bias):
    mu = jnp.mean(r, axis=-1, keepdims=True)
    c = r - mu
    var = jnp.mean(c * c, axis=-1, keepdims=True)
    return c * lax.rsqrt(var + LN_EPS) * gain + bias


def _silu(t):
    return t * jax.nn.sigmoid(t)


def _dot(a, b):
    return jnp.dot(a, b, preferred_element_type=F32)


def _swiglu(xb, win_ref, wout_ref, act_ref, rs):
    for j in range(D_FF // FFN_CHUNK):
        lo = j * FFN_CHUNK
        gate = _dot(xb, win_ref[:, lo:lo + FFN_CHUNK])
        up = _dot(xb, win_ref[:, D_FF + lo:D_FF + lo + FFN_CHUNK])
        act_ref[rs, lo:lo + FFN_CHUNK] = (_silu(gate) * up).astype(BF16)
    return _dot(act_ref[rs, :], wout_ref[...])


def _ffn_ln_kernel(alpha, x_ref, win_ref, wout_ref, g_ref, b_ref, o_ref, act_ref):
    for part in range(FFN_TILE // FFN_PART):
        rs = slice(part * FFN_PART, (part + 1) * FFN_PART)
        x = x_ref[rs, :]
        y = _swiglu(x.astype(BF16), win_ref, wout_ref, act_ref, rs)
        o_ref[rs, :] = _layer_norm(alpha * x + 0.5 * y, g_ref[...], b_ref[...])


def _ffn_ln(x2, w_in, w_out, gain, bias, alpha):
    rows = x2.shape[0]
    const = lambda i: (0, 0)
    return pl.pallas_call(
        functools.partial(_ffn_ln_kernel, alpha),
        out_shape=jax.ShapeDtypeStruct((rows, D_MODEL), F32),
        grid=(rows // FFN_TILE,),
        in_specs=[
            pl.BlockSpec((FFN_TILE, D_MODEL), lambda i: (i, 0)),
            pl.BlockSpec((D_MODEL, 2 * D_FF), const, pipeline_mode=pl.Buffered(1)),
            pl.BlockSpec((D_FF, D_MODEL), const, pipeline_mode=pl.Buffered(1)),
            pl.BlockSpec((1, D_MODEL), const),
            pl.BlockSpec((1, D_MODEL), const),
        ],
        out_specs=pl.BlockSpec((FFN_TILE, D_MODEL), lambda i: (i, 0)),
        scratch_shapes=[pltpu.VMEM((FFN_TILE, D_FF), BF16)],
        compiler_params=pltpu.CompilerParams(
            dimension_semantics=("arbitrary",), vmem_limit_bytes=_vmem_limit(56 << 20)),
        name="ffn_ln",
    )(x2, w_in, w_out, gain, bias)


_W_ROW_K = 0
_W_ROW_Z = KV_DIM
_W_ROW_DQKV = KV_DIM + DN_WIDTH
_W_ROW_COLS = KV_DIM + DN_WIDTH + 3 * DN_WIDTH
_W_T_Q = 0
_W_T_V = ATTN_WIDTH
_W_T_GATE = ATTN_WIDTH + KV_DIM
_W_T_ROWS = _W_T_GATE + GATE_ROWS


def _mixer_in_kernel(x_ref, xp_ref, xn_ref, wrow_ref, wt_ref, conv_ref, gp_ref, tri_ref,
                     qt_ref, k_ref, vt_ref, z_ref, dq_ref, dk_ref, dv_ref, gcol_ref, gct_ref, proj_ref):
    i = pl.program_id(1)
    last = pl.num_programs(1) - 1
    tile, halo = MIX_TILE, MIX_HALO
    prev = jnp.where(i > 0, xp_ref[0], 0.0)
    nxt = jnp.where(i < last, xn_ref[0], 0.0)
    xe = jnp.concatenate([prev, x_ref[0], nxt], axis=0).astype(BF16)
    xb = xe[halo:halo + tile]

    tproj = lax.dot_general(wt_ref[...], xb, (((1,), (1,)), ((), ())), preferred_element_type=F32)
    qt_ref[0] = tproj[_W_T_Q:_W_T_V].astype(BF16)
    vt_ref[0] = tproj[_W_T_V:_W_T_GATE].astype(BF16)
    gate_logits = tproj[_W_T_GATE:_W_T_ROWS]

    k_ref[0] = _dot(xb, wrow_ref[:, _W_ROW_K:_W_ROW_Z]).astype(BF16)
    z_ref[0] = _dot(xb, wrow_ref[:, _W_ROW_Z:_W_ROW_DQKV]).astype(BF16)

    for kind, o_ref in enumerate((dq_ref, dk_ref, dv_ref)):
        lo = _W_ROW_DQKV + kind * DN_WIDTH
        proj = _dot(xe, wrow_ref[:, lo:lo + DN_WIDTH])
        for h in range(DN_HEADS):
            proj_ref[h] = proj[:, h * DN_HEAD_DIM:(h + 1) * DN_HEAD_DIM]
        for h in range(DN_HEADS):
            col = kind * DN_WIDTH + h * DN_HEAD_DIM
            y = jnp.zeros((tile, DN_HEAD_DIM), F32)
            for t in range(CONV_WIDTH):
                first = halo + t - CONV_PAD
                xs = proj_ref[h, pl.ds(first, tile), :]
                y = y + conv_ref[t:t + 1, col:col + DN_HEAD_DIM] * xs
            s = _silu(y)
            if kind < 2:
                s = s * lax.rsqrt(jnp.sum(s * s, axis=-1, keepdims=True) + RMS_EPS)
            if kind == 0:
                s = s * (DN_HEAD_DIM ** -0.5)
            o_ref[0, h] = s.astype(BF16)

    row = lax.broadcasted_iota(jnp.int32, (GATE_ROWS, SCAN_BLOCK), 0)
    beta = jax.nn.sigmoid(gate_logits)
    t = gate_logits + gp_ref[:, 1:2]
    softplus = jnp.maximum(t, 0.0) + jnp.log1p(jnp.exp(-jnp.abs(t)))
    g = -jnp.exp(gp_ref[:, 0:1]) * softplus
    g_hi = g.astype(BF16)
    r1 = g - g_hi.astype(F32)
    g_mid = r1.astype(BF16)
    g_lo = (r1 - g_mid.astype(F32)).astype(BF16)
    is_fwd = row < 2 * DN_HEADS + DN_HEADS
    pad_rows = jnp.zeros((SCAN_BLOCK - GATE_ROWS, SCAN_BLOCK), F32)
    for r in range(tile // SCAN_BLOCK):
        ls = slice(r * SCAN_BLOCK, (r + 1) * SCAN_BLOCK)
        parts = _dot(jnp.concatenate([g_hi[:, ls], g_mid[:, ls], g_lo[:, ls]], axis=0), tri_ref[...])
        sums = parts[:GATE_ROWS] + parts[GATE_ROWS:2 * GATE_ROWS] + parts[2 * GATE_ROWS:]
        gc = jnp.where(is_fwd, sums[:, :SCAN_BLOCK], sums[:, SCAN_BLOCK:])
        gct_ref[0, r] = gc[2 * DN_HEADS:]
        both = jnp.where(row < 2 * DN_HEADS, beta[:, ls], gc)
        gcol_ref[0, ls, :] = jnp.concatenate([both, pad_rows], axis=0).T


def _mixer_in(x, w_row, w_t, conv_w, gate_params, tri):
    b, l, _ = x.shape
    per = MIX_TILE // MIX_HALO
    nh = l // MIX_HALO
    cur = lambda bi, i: (bi, i, 0)
    const2 = lambda bi, i: (0, 0)
    head_major = jax.ShapeDtypeStruct((b, DN_HEADS, l, DN_HEAD_DIM), BF16)
    head_spec = pl.BlockSpec((1, DN_HEADS, MIX_TILE, DN_HEAD_DIM), lambda bi, i: (bi, 0, i, 0))
    return pl.pallas_call(
        _mixer_in_kernel,
        out_shape=[
            jax.ShapeDtypeStruct((b, ATTN_WIDTH, l), BF16),
            jax.ShapeDtypeStruct((b, l, KV_DIM), BF16),
            jax.ShapeDtypeStruct((b, KV_DIM, l), BF16),
            jax.ShapeDtypeStruct((b, l, DN_WIDTH), BF16),
            head_major, head_major, head_major,
            jax.ShapeDtypeStruct((b, l, V7X_LANES), F32),
            jax.ShapeDtypeStruct((b, l // SCAN_BLOCK, 2 * DN_HEADS, SCAN_BLOCK), F32),
        ],
        grid=(b, l // MIX_TILE),
        in_specs=[
            pl.BlockSpec((1, MIX_TILE, D_MODEL), cur),
            pl.BlockSpec((1, MIX_HALO, D_MODEL), lambda bi, i: (bi, jnp.maximum(i * per - 1, 0), 0)),
            pl.BlockSpec((1, MIX_HALO, D_MODEL), lambda bi, i: (bi, jnp.minimum((i + 1) * per, nh - 1), 0)),
            pl.BlockSpec((D_MODEL, _W_ROW_COLS), const2, pipeline_mode=pl.Buffered(1)),
            pl.BlockSpec((_W_T_ROWS, D_MODEL), const2, pipeline_mode=pl.Buffered(1)),
            pl.BlockSpec((8, 3 * DN_WIDTH), const2),
            pl.BlockSpec((GATE_ROWS, V7X_LANES), const2),
            pl.BlockSpec((SCAN_BLOCK, 2 * SCAN_BLOCK), const2),
        ],
        out_specs=[
            pl.BlockSpec((1, ATTN_WIDTH, MIX_TILE), lambda bi, i: (bi, 0, i)),
            pl.BlockSpec((1, MIX_TILE, KV_DIM), cur),
            pl.BlockSpec((1, KV_DIM, MIX_TILE), lambda bi, i: (bi, 0, i)),
            pl.BlockSpec((1, MIX_TILE, DN_WIDTH), cur),
            head_spec, head_spec, head_spec,
            pl.BlockSpec((1, MIX_TILE, V7X_LANES), cur),
            pl.BlockSpec((1, MIX_TILE // SCAN_BLOCK, 2 * DN_HEADS, SCAN_BLOCK), lambda bi, i: (bi, i, 0, 0)),
        ],
        scratch_shapes=[pltpu.VMEM((DN_HEADS, MIX_TILE + 2 * MIX_HALO, DN_HEAD_DIM), F32)],
        compiler_params=pltpu.CompilerParams(
            dimension_semantics=("arbitrary", "arbitrary"), vmem_limit_bytes=_vmem_limit(48 << 20)),
        name="mixer_in",
    )(x, x, x, w_row, w_t, conv_w, gate_params, tri)


def _chunk_triangles():
    t = jnp.arange(SCAN_BLOCK)
    same = (t[:, None] // CHUNK) == (t[None, :] // CHUNK)
    prefix = same & (t[:, None] <= t[None, :])
    suffix = same & (t[:, None] >= t[None, :])
    return jnp.concatenate([prefix, suffix], axis=1).astype(BF16)


def _attn_kernel(qt_ref, kp_ref, kc_ref, kn_ref, vtp_ref, vtc_ref, vtn_ref,
                 bias_first_ref, bias_mid_ref, bias_last_ref, sink_ref, o_ref):
    qt = qt_ref[0] * jnp.asarray(ATTN_HEAD_DIM ** -0.5, BF16)
    kcat = jnp.concatenate([kp_ref[0], kc_ref[0], kn_ref[0]], axis=0)
    vtcat = jnp.concatenate([vtp_ref[0], vtc_ref[0], vtn_ref[0]], axis=1)
    no_head = jnp.zeros((ATTN_HEAD_DIM, BLOCK), BF16)
    items = [(j, h) for j in range(ATTN_QBLOCKS) for h in range(ATTN_KV_HEADS)]
    scores, outs = {}, {}
    for j, h in items:
        cols = []
        for g in range(ATTN_GROUP):
            lo = (h * ATTN_GROUP + g) * ATTN_HEAD_DIM
            qhg = qt[lo:lo + ATTN_HEAD_DIM, j * BLOCK:(j + 1) * BLOCK]
            cols.append(jnp.concatenate([qhg, no_head] if h == 0 else [no_head, qhg], axis=0))
        scores[j, h] = _dot(kcat[j * BLOCK:(j + 3) * BLOCK], jnp.concatenate(cols, axis=1))
    for j, h in items:
        bias_ref = bias_first_ref if j == 0 else (bias_last_ref if j == ATTN_QBLOCKS - 1 else bias_mid_ref)
        logits = scores[j, h] + bias_ref[0, h]
        sink = sink_ref[h][0:1, :]
        m = jnp.maximum(jnp.max(logits, axis=0, keepdims=True), sink)
        p = jnp.exp(logits - m)
        denom = jnp.sum(p, axis=0, keepdims=True) + jnp.exp(sink - m)
        vt_h = vtcat[h * ATTN_HEAD_DIM:(h + 1) * ATTN_HEAD_DIM, j * BLOCK:(j + 3) * BLOCK]
        pv = _dot(vt_h, p.astype(BF16)) / denom
        outs[j, h] = [pv[:, g * BLOCK:(g + 1) * BLOCK] for g in range(ATTN_GROUP)]
    for j in range(ATTN_QBLOCKS):
        pieces = [piece for h in range(ATTN_KV_HEADS) for piece in outs[j, h]]
        o_ref[0, j * BLOCK:(j + 1) * BLOCK, :] = jnp.concatenate(pieces, axis=0).T.astype(BF16)


def _attention(qt, k, vt, bias, sink):
    b, _, l = qt.shape
    qb = ATTN_QBLOCKS
    nb = l // BLOCK
    steps = nb // qb
    before = lambda n: jnp.maximum(n * qb - 1, 0)
    after = lambda n: jnp.minimum((n + 1) * qb, nb - 1)
    no_prev = lambda n: (n == 0).astype(jnp.int32)
    no_next = lambda n: 2 * (n == steps - 1).astype(jnp.int32)
    first = lambda bi, n: (no_prev(n) + (no_next(n) if qb == 1 else 0), 0, 0, 0)
    mid = lambda bi, n: (0, 0, 0, 0)
    last = lambda bi, n: (no_next(n), 0, 0, 0)
    k_halo = lambda f: pl.BlockSpec((1, BLOCK, KV_DIM), lambda bi, n: (bi, f(n), 0))
    vt_halo = lambda f: pl.BlockSpec((1, KV_DIM, BLOCK), lambda bi, n: (bi, 0, f(n)))
    bias_spec = lambda imap: pl.BlockSpec((1, ATTN_KV_HEADS, 3 * BLOCK, ATTN_GROUP * BLOCK), imap)
    return pl.pallas_call(
        _attn_kernel,
        out_shape=jax.ShapeDtypeStruct((b, l, ATTN_WIDTH), BF16),
        grid=(b, steps),
        in_specs=[
            pl.BlockSpec((1, ATTN_WIDTH, qb * BLOCK), lambda bi, n: (bi, 0, n)),
            k_halo(before), pl.BlockSpec((1, qb * BLOCK, KV_DIM), lambda bi, n: (bi, n, 0)), k_halo(after),
            vt_halo(before), pl.BlockSpec((1, KV_DIM, qb * BLOCK), lambda bi, n: (bi, 0, n)), vt_halo(after),
            bias_spec(first), bias_spec(mid), bias_spec(last),
            pl.BlockSpec((ATTN_KV_HEADS, 8, ATTN_GROUP * BLOCK), lambda bi, n: (0, 0, 0)),
        ],
        out_specs=pl.BlockSpec((1, qb * BLOCK, ATTN_WIDTH), lambda bi, n: (bi, n, 0)),
        compiler_params=pltpu.CompilerParams(
            dimension_semantics=("arbitrary", "arbitrary"), vmem_limit_bytes=_vmem_limit(40 << 20)),
        name="swa_attention",
    )(qt, k, k, k, vt, vt, vt, bias, bias, bias, sink)


def _attn_tables(attn_sink):
    slopes = 2.0 ** (-8.0 * jnp.arange(1, ATTN_HEADS + 1, dtype=F32) / ATTN_HEADS)
    kj = jnp.arange(3 * BLOCK)[:, None]
    qi = jnp.arange(BLOCK)[None, :]
    dist = jnp.abs(qi - kj + BLOCK)
    bias = jnp.where(dist <= WINDOW, -slopes[:, None, None] * dist.astype(F32)[None], NEG_BIG)
    bias = bias.reshape(ATTN_KV_HEADS, ATTN_GROUP, 3 * BLOCK, BLOCK).transpose(0, 2, 1, 3)
    bias = bias.reshape(ATTN_KV_HEADS, 3 * BLOCK, ATTN_GROUP * BLOCK)
    no_prev = (kj < BLOCK)[None]
    no_next = (kj >= 2 * BLOCK)[None]
    bias = jnp.stack([jnp.where(hidden, NEG_BIG, bias)
                      for hidden in (jnp.zeros_like(no_prev), no_prev, no_next, no_prev | no_next)])
    sink = jnp.broadcast_to(attn_sink.astype(F32).reshape(ATTN_KV_HEADS, 1, ATTN_GROUP, 1),
                            (ATTN_KV_HEADS, 8, ATTN_GROUP, BLOCK))
    return bias, sink.reshape(ATTN_KV_HEADS, 8, ATTN_GROUP * BLOCK)


def _dn_scan_kernel(qf_ref, kf_ref, vf_ref, qb_ref, kb_ref, vb_ref,
                    gcolf_ref, gctf_ref, gcolb_ref, gctb_ref,
                    of_ref, ob_ref, state_ref):
    @pl.when(pl.program_id(1) == 0)
    def _():
        state_ref[...] = jnp.zeros_like(state_ref)

    n = SCAN_BLOCK
    row = lax.broadcasted_iota(jnp.int32, (n, n), 0)
    col = lax.broadcasted_iota(jnp.int32, (n, n), 1)
    same_chunk = (row // CHUNK) == (col // CHUNK)
    eye = (row == col).astype(F32)
    off_diag = row != col
    directions = (
        (qf_ref, kf_ref, vf_ref, gcolf_ref, gctf_ref, of_ref, same_chunk & (row >= col)),
        (qb_ref, kb_ref, vb_ref, gcolb_ref, gctb_ref, ob_ref, same_chunk & (row <= col)),
    )
    probs = [(d, h, b) for b in range(SCAN_STEP_BLOCKS) for d in range(2) for h in range(DN_HEADS)]
    kk, qq, vv, beta, gcol, egc, p, y, attn = ({} for _ in range(9))
    for pr in probs:
        d, h, b = pr
        q_ref, k_ref, v_ref, gcol_ref, gct_ref, _, incl = directions[d]
        lane = d * DN_HEADS + h
        blk = slice(b * n, (b + 1) * n)
        kk[pr] = k_ref[0, h, blk]
        qq[pr] = q_ref[0, h, blk]
        vv[pr] = v_ref[0, h, blk]
        beta[pr] = gcol_ref[0, blk][:, lane:lane + 1]
        gcol[pr] = gcol_ref[0, blk][:, 2 * DN_HEADS + lane:2 * DN_HEADS + lane + 1]
        grow = gct_ref[0, b][lane:lane + 1, :]
        gram = lax.dot_general(jnp.concatenate([kk[pr], qq[pr]], axis=0), kk[pr],
                               (((1,), (1,)), ((), ())), preferred_element_type=F32)
        decay = jnp.exp(jnp.where(incl, gcol[pr] - grow, NEG_BIG))
        neg_n = -(beta[pr] * gram[:n] * jnp.where(off_diag, decay, 0.0))
        attn[pr] = (gram[n:] * decay).astype(BF16)
        p[pr] = eye + neg_n
        y[pr] = neg_n.astype(BF16)
    for pr in probs:
        y[pr] = _dot(y[pr], y[pr])
    for _ in range(4):
        for pr in probs:
            yb = y[pr].astype(BF16)
            r = _dot(yb, jnp.concatenate([p[pr].astype(BF16), yb], axis=1))
            p[pr] = p[pr] + r[:, :n]
            y[pr] = r[:, n:]
    for pr in probs:
        p[pr] = p[pr] + _dot(y[pr].astype(BF16), p[pr].astype(BF16))
    uwb, auw, q_eff = {}, {}, {}
    for pr in probs:
        egc[pr] = jnp.exp(gcol[pr])
        rhs = jnp.concatenate([vv[pr].astype(F32) * beta[pr],
                               kk[pr].astype(F32) * (beta[pr] * egc[pr])], axis=1)
        uwb[pr] = _dot(p[pr].astype(BF16), rhs.astype(BF16)).astype(BF16)
    for pr in probs:
        auw[pr] = _dot(attn[pr], uwb[pr])
        q_eff[pr] = qq[pr].astype(F32) * egc[pr] - auw[pr][:, n:]
    chunks_per_block = n // CHUNK
    steps = SCAN_STEP_BLOCKS * chunks_per_block

    def locate(d, step):
        pos = step if d == 0 else steps - 1 - step
        return divmod(pos, chunks_per_block)

    bm, g_last = {}, {}
    for step in range(steps):
        for d in range(2):
            for h in range(DN_HEADS):
                b, c = locate(d, step)
                pr = (d, h, b)
                rs = slice(c * CHUNK, (c + 1) * CHUNK)
                end = c * CHUNK + (CHUNK - 1 if d == 0 else 0)
                g_last[pr, c] = gcol[pr][end:end + 1]
                k_dec = kk[pr][rs].astype(F32) * jnp.exp(g_last[pr, c] - gcol[pr][rs])
                bm[pr, c] = lax.dot_general(k_dec.astype(BF16), uwb[pr][rs], (((0,), (0,)), ((), ())),
                                            preferred_element_type=F32)
    for step in range(steps):
        for d in range(2):
            o_ref = directions[d][5]
            for h in range(DN_HEADS):
                b, c = locate(d, step)
                pr = (d, h, b)
                rs = slice(c * CHUNK, (c + 1) * CHUNK)
                lane = d * DN_HEADS + h
                s_old = state_ref[lane]
                lhs = jnp.concatenate([bm[pr, c][:, n:], q_eff[pr][rs]], axis=0).astype(BF16)
                ms = _dot(lhs, s_old.astype(BF16))
                state_ref[lane] = jnp.exp(g_last[pr, c]) * s_old - ms[:n] + bm[pr, c][:, :n]
                o_ref[0, b * n + c * CHUNK:b * n + (c + 1) * CHUNK, h * DN_HEAD_DIM:(h + 1) * DN_HEAD_DIM] = (
                    ms[n:] + auw[pr][rs, :n]).astype(BF16)


def _dn_scan(q, k, v, gcol, gct):
    b, _, l, _ = q.shape
    span = SCAN_STEP_BLOCKS * SCAN_BLOCK
    nb = l // span
    fwd4 = lambda bi, i: (bi, 0, i, 0)
    bwd4 = lambda bi, i: (bi, 0, nb - 1 - i, 0)
    fwd3 = lambda bi, i: (bi, i, 0)
    bwd3 = lambda bi, i: (bi, nb - 1 - i, 0)
    fwdt = lambda bi, i: (bi, i, 0, 0)
    bwdt = lambda bi, i: (bi, nb - 1 - i, 0, 0)
    qkv_spec = lambda imap: pl.BlockSpec((1, DN_HEADS, span, DN_HEAD_DIM), imap)
    gate_spec = lambda imap: pl.BlockSpec((1, span, V7X_LANES), imap)
    gct_spec = lambda imap: pl.BlockSpec((1, SCAN_STEP_BLOCKS, 2 * DN_HEADS, SCAN_BLOCK), imap)
    out = jax.ShapeDtypeStruct((b, l, DN_WIDTH), BF16)
    return pl.pallas_call(
        _dn_scan_kernel,
        out_shape=[out, out],
        grid=(b, nb),
        in_specs=[qkv_spec(fwd4), qkv_spec(fwd4), qkv_spec(fwd4),
                  qkv_spec(bwd4), qkv_spec(bwd4), qkv_spec(bwd4),
                  gate_spec(fwd3), gct_spec(fwdt), gate_spec(bwd3), gct_spec(bwdt)],
        out_specs=[pl.BlockSpec((1, span, DN_WIDTH), fwd3),
                   pl.BlockSpec((1, span, DN_WIDTH), bwd3)],
        scratch_shapes=[pltpu.VMEM((2 * DN_HEADS, DN_HEAD_DIM, DN_HEAD_DIM), F32)],
        compiler_params=pltpu.CompilerParams(
            dimension_semantics=("arbitrary", "arbitrary"), vmem_limit_bytes=_vmem_limit(40 << 20)),
        name="dn_scan",
    )(q, k, v, q, k, v, gcol, gct, gcol, gct)


def _mix_ffn_kernel(alpha, oa_ref, of_ref, ob_ref, z_ref, x_ref, w_ref, ng_ref, g2_ref, b2_ref,
                    win_ref, wout_ref, g3_ref, b3_ref, o_ref, dn_ref, act_ref):
    parts = [slice(i * FFN_PART, (i + 1) * FFN_PART) for i in range(FFN_TILE // FFN_PART)]
    mixed = []
    for rs in parts:
        for h in range(DN_HEADS):
            sl = slice(h * DN_HEAD_DIM, (h + 1) * DN_HEAD_DIM)
            o = of_ref[rs, sl].astype(F32) + ob_ref[rs, sl].astype(F32)
            ms = jnp.mean(o * o, axis=-1, keepdims=True)
            gate = _silu(z_ref[rs, sl].astype(F32))
            dn_ref[rs, sl] = (o * lax.rsqrt(ms + RMS_EPS) * ng_ref[...] * gate).astype(BF16)
        y = _dot(oa_ref[rs, :], w_ref[:ATTN_WIDTH, :]) + _dot(dn_ref[rs, :], w_ref[ATTN_WIDTH:, :])
        mixed.append(_layer_norm(alpha * x_ref[rs, :] + y, g2_ref[...], b2_ref[...]))
    for rs, x in zip(parts, mixed):
        y = _swiglu(x.astype(BF16), win_ref, wout_ref, act_ref, rs)
        o_ref[rs, :] = _layer_norm(alpha * x + 0.5 * y, g3_ref[...], b3_ref[...])


def _mix_ffn(oa, of, ob, z, x2, w_out, norm_gain, gain2, bias2, w_in, w_ffn_out, gain3, bias3, alpha):
    rows = x2.shape[0]
    row = lambda i: (i, 0)
    const = lambda i: (0, 0)
    resident = lambda shape: pl.BlockSpec(shape, const, pipeline_mode=pl.Buffered(1))
    return pl.pallas_call(
        functools.partial(_mix_ffn_kernel, alpha),
        out_shape=jax.ShapeDtypeStruct((rows, D_MODEL), F32),
        grid=(rows // FFN_TILE,),
        in_specs=[
            pl.BlockSpec((FFN_TILE, ATTN_WIDTH), row),
            pl.BlockSpec((FFN_TILE, DN_WIDTH), row),
            pl.BlockSpec((FFN_TILE, DN_WIDTH), row),
            pl.BlockSpec((FFN_TILE, DN_WIDTH), row),
            pl.BlockSpec((FFN_TILE, D_MODEL), row),
            resident((D_MODEL, D_MODEL)),
            pl.BlockSpec((1, DN_HEAD_DIM), const),
            pl.BlockSpec((1, D_MODEL), const),
            pl.BlockSpec((1, D_MODEL), const),
            resident((D_MODEL, 2 * D_FF)),
            resident((D_FF, D_MODEL)),
            pl.BlockSpec((1, D_MODEL), const),
            pl.BlockSpec((1, D_MODEL), const),
        ],
        out_specs=pl.BlockSpec((FFN_TILE, D_MODEL), row),
        scratch_shapes=[pltpu.VMEM((FFN_TILE, DN_WIDTH), BF16), pltpu.VMEM((FFN_TILE, D_FF), BF16)],
        compiler_params=pltpu.CompilerParams(
            dimension_semantics=("arbitrary",), vmem_limit_bytes=_vmem_limit(60 << 20)),
        name="mix_ffn",
    )(oa, of, ob, z, x2, w_out, norm_gain, gain2, bias2, w_in, w_ffn_out, gain3, bias3)


def _pack_mixer_weights(w_in):
    aq = w_in[:, :ATTN_WIDTH]
    ak = w_in[:, ATTN_WIDTH:ATTN_WIDTH + KV_DIM]
    av = w_in[:, ATTN_WIDTH + KV_DIM:ATTN_WIDTH + 2 * KV_DIM]
    dqkv = w_in[:, ATTN_WIDTH + 2 * KV_DIM:ATTN_WIDTH + 2 * KV_DIM + 3 * DN_WIDTH]
    z = w_in[:, ATTN_WIDTH + 2 * KV_DIM + 3 * DN_WIDTH:OFF_B]
    gates = w_in[:, OFF_B:]
    w_row = jnp.concatenate([ak, z, dqkv], axis=1).astype(BF16)
    w_t = jnp.concatenate([aq, av, gates], axis=1).T.astype(BF16)
    return w_row, w_t


def _gate_params(a_log, dt_bias):
    cols = jnp.stack([a_log.reshape(-1), dt_bias.reshape(-1)], axis=1).astype(F32)
    return jnp.pad(cols, ((GATE_ROWS - cols.shape[0], 0), (0, V7X_LANES - cols.shape[1])))


def kernel(x_prompt, x_sample, ffn1_w_in, ffn1_w_out, w_in, conv_w, attn_sink, dn_a_log, dn_dt_bias,
           dn_norm_gain, w_out, ffn2_w_in, ffn2_w_out, ln_gain, ln_bias):
    depth = w_in.shape[0]
    alpha = (2.0 * depth) ** 0.25
    tri = _chunk_triangles()
    layers = []
    for i in range(depth):
        bias, sink = _attn_tables(attn_sink[i])
        w_row, w_t = _pack_mixer_weights(w_in[i])
        layers.append(dict(
            ffn1_in=ffn1_w_in[i].astype(BF16), ffn1_out=ffn1_w_out[i].astype(BF16),
            ffn2_in=ffn2_w_in[i].astype(BF16), ffn2_out=ffn2_w_out[i].astype(BF16),
            w_row=w_row, w_t=w_t, w_out=w_out[i].astype(BF16),
            conv=jnp.pad(conv_w[i].astype(F32), ((0, 8 - CONV_WIDTH), (0, 0))),
            gate=_gate_params(dn_a_log[i], dn_dt_bias[i]),
            norm_gain=dn_norm_gain[i].astype(F32).reshape(1, DN_HEAD_DIM),
            bias=bias, sink=sink,
            ln_g=ln_gain[i].astype(F32).reshape(3, 1, D_MODEL),
            ln_b=ln_bias[i].astype(F32).reshape(3, 1, D_MODEL),
        ))

    def trunk(x):
        b, l, _ = x.shape
        x2 = x.reshape(b * l, D_MODEL)
        for p in layers:
            x2 = _ffn_ln(x2, p["ffn1_in"], p["ffn1_out"], p["ln_g"][0], p["ln_b"][0], alpha)
            qt, ak, vt, z, dq, dk, dv, gcol, gct = _mixer_in(
                x2.reshape(b, l, D_MODEL), p["w_row"], p["w_t"], p["conv"], p["gate"], tri)
            oa = _attention(qt, ak, vt, p["bias"], p["sink"])
            of, ob = _dn_scan(dq, dk, dv, gcol, gct)
            x2 = _mix_ffn(oa.reshape(b * l, ATTN_WIDTH), of.reshape(b * l, DN_WIDTH),
                          ob.reshape(b * l, DN_WIDTH), z.reshape(b * l, DN_WIDTH), x2, p["w_out"],
                          p["norm_gain"], p["ln_g"][1], p["ln_b"][1],
                          p["ffn2_in"], p["ffn2_out"], p["ln_g"][2], p["ln_b"][2], alpha)
        return x2.reshape(b, l, D_MODEL)

    return (trunk(x_prompt), trunk(x_sample))
```

```python
import functools

import jax
import jax.numpy as jnp
from jax import lax
from jax.experimental import pallas as pl
from jax.experimental.pallas import tpu as pltpu

F32 = jnp.float32
BF16 = jnp.bfloat16

D_MODEL = 1024
ATTN_WIDTH = 512
ATTN_HEAD_DIM = 64
ATTN_HEADS = 8
ATTN_KV_HEADS = 2
ATTN_GROUP = ATTN_HEADS // ATTN_KV_HEADS
KV_DIM = ATTN_KV_HEADS * ATTN_HEAD_DIM
WINDOW = 128
BLOCK = 128
DN_WIDTH = 512
DN_HEAD_DIM = 128
DN_HEADS = DN_WIDTH // DN_HEAD_DIM
CONV_WIDTH = 5
CONV_PAD = CONV_WIDTH // 2
CHUNK = 64
D_FF = 2816
LN_EPS = 1e-5
RMS_EPS = 1e-6
OFF_B = ATTN_WIDTH + 2 * KV_DIM + 4 * DN_WIDTH
OFF_A = OFF_B + 2 * DN_HEADS

V7X_LANES = 128
V7X_BF16_SUBLANES = 16
V7X_VMEM_BYTES = 64 * 1024 * 1024

FFN_TILE = 1024
FFN_PART = 256
FFN_CHUNK = 256
ATTN_QBLOCKS = 16
MIX_TILE = 1024
MIX_HALO = V7X_BF16_SUBLANES
SCAN_BLOCK = 2 * CHUNK
SCAN_STEP_BLOCKS = 8
GATE_ROWS = 4 * DN_HEADS
NEG_BIG = -1e30


def _vmem_limit(nbytes):
    return int(min(nbytes, V7X_VMEM_BYTES - 4 * 1024 * 1024))


def _layer_norm(r, gain, bias):
    mu = jnp.mean(r, axis=-1, keepdims=True)
    c = r - mu
    var = jnp.mean(c * c, axis=-1, keepdims=True)
    return c * lax.rsqrt(var + LN_EPS) * gain + bias


def _silu(t):
    return t * jax.nn.sigmoid(t)


def _dot(a, b):
    return jnp.dot(a, b, preferred_element_type=F32)


def _swiglu(xb, win_ref, wout_ref, act_ref, rs):
    for j in range(D_FF // FFN_CHUNK):
        lo = j * FFN_CHUNK
        gate = _dot(xb, win_ref[:, lo:lo + FFN_CHUNK])
        up = _dot(xb, win_ref[:, D_FF + lo:D_FF + lo + FFN_CHUNK])
        act_ref[rs, lo:lo + FFN_CHUNK] = (_silu(gate) * up).astype(BF16)
    return _dot(act_ref[rs, :], wout_ref[...])


def _ffn_ln_kernel(alpha, x_ref, win_ref, wout_ref, g_ref, b_ref, o_ref, act_ref):
    for part in range(FFN_TILE // FFN_PART):
        rs = slice(part * FFN_PART, (part + 1) * FFN_PART)
        x = x_ref[rs, :]
        y = _swiglu(x.astype(BF16), win_ref, wout_ref, act_ref, rs)
        o_ref[rs, :] = _layer_norm(alpha * x + 0.5 * y, g_ref[...], b_ref[...])


def _ffn_ln(x2, w_in, w_out, gain, bias, alpha):
    rows = x2.shape[0]
    const = lambda i: (0, 0)
    return pl.pallas_call(
        functools.partial(_ffn_ln_kernel, alpha),
        out_shape=jax.ShapeDtypeStruct((rows, D_MODEL), F32),
        grid=(rows // FFN_TILE,),
        in_specs=[
            pl.BlockSpec((FFN_TILE, D_MODEL), lambda i: (i, 0)),
            pl.BlockSpec((D_MODEL, 2 * D_FF), const, pipeline_mode=pl.Buffered(1)),
            pl.BlockSpec((D_FF, D_MODEL), const, pipeline_mode=pl.Buffered(1)),
            pl.BlockSpec((1, D_MODEL), const),
            pl.BlockSpec((1, D_MODEL), const),
        ],
        out_specs=pl.BlockSpec((FFN_TILE, D_MODEL), lambda i: (i, 0)),
        scratch_shapes=[pltpu.VMEM((FFN_TILE, D_FF), BF16)],
        compiler_params=pltpu.CompilerParams(
            dimension_semantics=("arbitrary",), vmem_limit_bytes=_vmem_limit(56 << 20)),
        name="ffn_ln",
    )(x2, w_in, w_out, gain, bias)


_W_ROW_K = 0
_W_ROW_Z = KV_DIM
_W_ROW_DQKV = KV_DIM + DN_WIDTH
_W_ROW_COLS = KV_DIM + DN_WIDTH + 3 * DN_WIDTH
_W_T_Q = 0
_W_T_V = ATTN_WIDTH
_W_T_GATE = ATTN_WIDTH + KV_DIM
_W_T_ROWS = _W_T_GATE + GATE_ROWS


def _mixer_in_kernel(x_ref, xp_ref, xn_ref, wrow_ref, wt_ref, conv_ref, gp_ref, tri_ref,
                     qt_ref, k_ref, vt_ref, z_ref, dq_ref, dk_ref, dv_ref, gcol_ref, gct_ref, proj_ref):
    i = pl.program_id(1)
    last = pl.num_programs(1) - 1
    tile, halo = MIX_TILE, MIX_HALO
    prev = jnp.where(i > 0, xp_ref[0], 0.0)
    nxt = jnp.where(i < last, xn_ref[0], 0.0)
    xe = jnp.concatenate([prev, x_ref[0], nxt], axis=0).astype(BF16)
    xb = xe[halo:halo + tile]

    tproj = lax.dot_general(wt_ref[...], xb, (((1,), (1,)), ((), ())), preferred_element_type=F32)
    qt_ref[0] = tproj[_W_T_Q:_W_T_V].astype(BF16)
    vt_ref[0] = tproj[_W_T_V:_W_T_GATE].astype(BF16)
    gate_logits = tproj[_W_T_GATE:_W_T_ROWS]

    k_ref[0] = _dot(xb, wrow_ref[:, _W_ROW_K:_W_ROW_Z]).astype(BF16)
    z_ref[0] = _dot(xb, wrow_ref[:, _W_ROW_Z:_W_ROW_DQKV]).astype(BF16)

    for kind, o_ref in enumerate((dq_ref, dk_ref, dv_ref)):
        lo = _W_ROW_DQKV + kind * DN_WIDTH
        proj = _dot(xe, wrow_ref[:, lo:lo + DN_WIDTH])
        for h in range(DN_HEADS):
            proj_ref[h] = proj[:, h * DN_HEAD_DIM:(h + 1) * DN_HEAD_DIM]
        for h in range(DN_HEADS):
            col = kind * DN_WIDTH + h * DN_HEAD_DIM
            y = jnp.zeros((tile, DN_HEAD_DIM), F32)
            for t in range(CONV_WIDTH):
                first = halo + t - CONV_PAD
                xs = proj_ref[h, pl.ds(first, tile), :]
                y = y + conv_ref[t:t + 1, col:col + DN_HEAD_DIM] * xs
            s = _silu(y)
            if kind < 2:
                s = s * lax.rsqrt(jnp.sum(s * s, axis=-1, keepdims=True) + RMS_EPS)
            if kind == 0:
                s = s * (DN_HEAD_DIM ** -0.5)
            o_ref[0, h] = s.astype(BF16)

    row = lax.broadcasted_iota(jnp.int32, (GATE_ROWS, SCAN_BLOCK), 0)
    beta = jax.nn.sigmoid(gate_logits)
    t = gate_logits + gp_ref[:, 1:2]
    softplus = jnp.maximum(t, 0.0) + jnp.log1p(jnp.exp(-jnp.abs(t)))
    g = -jnp.exp(gp_ref[:, 0:1]) * softplus
    g_hi = g.astype(BF16)
    r1 = g - g_hi.astype(F32)
    g_mid = r1.astype(BF16)
    g_lo = (r1 - g_mid.astype(F32)).astype(BF16)
    is_fwd = row < 2 * DN_HEADS + DN_HEADS
    pad_rows = jnp.zeros((SCAN_BLOCK - GATE_ROWS, SCAN_BLOCK), F32)
    for r in range(tile // SCAN_BLOCK):
        ls = slice(r * SCAN_BLOCK, (r + 1) * SCAN_BLOCK)
        parts = _dot(jnp.concatenate([g_hi[:, ls], g_mid[:, ls], g_lo[:, ls]], axis=0), tri_ref[...])
        sums = parts[:GATE_ROWS] + parts[GATE_ROWS:2 * GATE_ROWS] + parts[2 * GATE_ROWS:]
        gc = jnp.where(is_fwd, sums[:, :SCAN_BLOCK], sums[:, SCAN_BLOCK:])
        gct_ref[0, r] = gc[2 * DN_HEADS:]
        both = jnp.where(row < 2 * DN_HEADS, beta[:, ls], gc)
        gcol_ref[0, ls, :] = jnp.concatenate([both, pad_rows], axis=0).T


def _mixer_in(x, w_row, w_t, conv_w, gate_params, tri):
    b, l, _ = x.shape
    per = MIX_TILE // MIX_HALO
    nh = l // MIX_HALO
    cur = lambda bi, i: (bi, i, 0)
    const2 = lambda bi, i: (0, 0)
    head_major = jax.ShapeDtypeStruct((b, DN_HEADS, l, DN_HEAD_DIM), BF16)
    head_spec = pl.BlockSpec((1, DN_HEADS, MIX_TILE, DN_HEAD_DIM), lambda bi, i: (bi, 0, i, 0))
    return pl.pallas_call(
        _mixer_in_kernel,
        out_shape=[
            jax.ShapeDtypeStruct((b, ATTN_WIDTH, l), BF16),
            jax.ShapeDtypeStruct((b, l, KV_DIM), BF16),
            jax.ShapeDtypeStruct((b, KV_DIM, l), BF16),
            jax.ShapeDtypeStruct((b, l, DN_WIDTH), BF16),
            head_major, head_major, head_major,
            jax.ShapeDtypeStruct((b, l, V7X_LANES), F32),
            jax.ShapeDtypeStruct((b, l // SCAN_BLOCK, 2 * DN_HEADS, SCAN_BLOCK), F32),
        ],
        grid=(b, l // MIX_TILE),
        in_specs=[
            pl.BlockSpec((1, MIX_TILE, D_MODEL), cur),
            pl.BlockSpec((1, MIX_HALO, D_MODEL), lambda bi, i: (bi, jnp.maximum(i * per - 1, 0), 0)),
            pl.BlockSpec((1, MIX_HALO, D_MODEL), lambda bi, i: (bi, jnp.minimum((i + 1) * per, nh - 1), 0)),
            pl.BlockSpec((D_MODEL, _W_ROW_COLS), const2, pipeline_mode=pl.Buffered(1)),
            pl.BlockSpec((_W_T_ROWS, D_MODEL), const2, pipeline_mode=pl.Buffered(1)),
            pl.BlockSpec((8, 3 * DN_WIDTH), const2),
            pl.BlockSpec((GATE_ROWS, V7X_LANES), const2),
            pl.BlockSpec((SCAN_BLOCK, 2 * SCAN_BLOCK), const2),
        ],
        out_specs=[
            pl.BlockSpec((1, ATTN_WIDTH, MIX_TILE), lambda bi, i: (bi, 0, i)),
            pl.BlockSpec((1, MIX_TILE, KV_DIM), cur),
            pl.BlockSpec((1, KV_DIM, MIX_TILE), lambda bi, i: (bi, 0, i)),
            pl.BlockSpec((1, MIX_TILE, DN_WIDTH), cur),
            head_spec, head_spec, head_spec,
            pl.BlockSpec((1, MIX_TILE, V7X_LANES), cur),
            pl.BlockSpec((1, MIX_TILE // SCAN_BLOCK, 2 * DN_HEADS, SCAN_BLOCK), lambda bi, i: (bi, i, 0, 0)),
        ],
        scratch_shapes=[pltpu.VMEM((DN_HEADS, MIX_TILE + 2 * MIX_HALO, DN_HEAD_DIM), F32)],
        compiler_params=pltpu.CompilerParams(
            dimension_semantics=("arbitrary", "arbitrary"), vmem_limit_bytes=_vmem_limit(48 << 20)),
        name="mixer_in",
    )(x, x, x, w_row, w_t, conv_w, gate_params, tri)


def _chunk_triangles():
    t = jnp.arange(SCAN_BLOCK)
    same = (t[:, None] // CHUNK) == (t[None, :] // CHUNK)
    prefix = same & (t[:, None] <= t[None, :])
    suffix = same & (t[:, None] >= t[None, :])
    return jnp.concatenate([prefix, suffix], axis=1).astype(BF16)


def _attn_kernel(qt_ref, kp_ref, kc_ref, kn_ref, vtp_ref, vtc_ref, vtn_ref,
                 bias_first_ref, bias_mid_ref, bias_last_ref, sink_ref, o_ref):
    qt = qt_ref[0] * jnp.asarray(ATTN_HEAD_DIM ** -0.5, BF16)
    kcat = jnp.concatenate([kp_ref[0], kc_ref[0], kn_ref[0]], axis=0)
    vtcat = jnp.concatenate([vtp_ref[0], vtc_ref[0], vtn_ref[0]], axis=1)
    no_head = jnp.zeros((ATTN_HEAD_DIM, BLOCK), BF16)
    items = [(j, h) for j in range(ATTN_QBLOCKS) for h in range(ATTN_KV_HEADS)]
    scores, outs = {}, {}
    for j, h in items:
        cols = []
        for g in range(ATTN_GROUP):
            lo = (h * ATTN_GROUP + g) * ATTN_HEAD_DIM
            qhg = qt[lo:lo + ATTN_HEAD_DIM, j * BLOCK:(j + 1) * BLOCK]
            cols.append(jnp.concatenate([qhg, no_head] if h == 0 else [no_head, qhg], axis=0))
        scores[j, h] = _dot(kcat[j * BLOCK:(j + 3) * BLOCK], jnp.concatenate(cols, axis=1))
    for j, h in items:
        bias_ref = bias_first_ref if j == 0 else (bias_last_ref if j == ATTN_QBLOCKS - 1 else bias_mid_ref)
        logits = scores[j, h] + bias_ref[0, h]
        sink = sink_ref[h][0:1, :]
        m = jnp.maximum(jnp.max(logits, axis=0, keepdims=True), sink)
        p = jnp.exp(logits - m)
        denom = jnp.sum(p, axis=0, keepdims=True) + jnp.exp(sink - m)
        vt_h = vtcat[h * ATTN_HEAD_DIM:(h + 1) * ATTN_HEAD_DIM, j * BLOCK:(j + 3) * BLOCK]
        pv = _dot(vt_h, p.astype(BF16)) / denom
        outs[j, h] = [pv[:, g * BLOCK:(g + 1) * BLOCK] for g in range(ATTN_GROUP)]
    for j in range(ATTN_QBLOCKS):
        pieces = [piece for h in range(ATTN_KV_HEADS) for piece in outs[j, h]]
        o_ref[0, j * BLOCK:(j + 1) * BLOCK, :] = jnp.concatenate(pieces, axis=0).T.astype(BF16)


def _attention(qt, k, vt, bias, sink):
    b, _, l = qt.shape
    qb = ATTN_QBLOCKS
    nb = l // BLOCK
    steps = nb // qb
    before = lambda n: jnp.maximum(n * qb - 1, 0)
    after = lambda n: jnp.minimum((n + 1) * qb, nb - 1)
    no_prev = lambda n: (n == 0).astype(jnp.int32)
    no_next = lambda n: 2 * (n == steps - 1).astype(jnp.int32)
    first = lambda bi, n: (no_prev(n) + (no_next(n) if qb == 1 else 0), 0, 0, 0)
    mid = lambda bi, n: (0, 0, 0, 0)
    last = lambda bi, n: (no_next(n), 0, 0, 0)
    k_halo = lambda f: pl.BlockSpec((1, BLOCK, KV_DIM), lambda bi, n: (bi, f(n), 0))
    vt_halo = lambda f: pl.BlockSpec((1, KV_DIM, BLOCK), lambda bi, n: (bi, 0, f(n)))
    bias_spec = lambda imap: pl.BlockSpec((1, ATTN_KV_HEADS, 3 * BLOCK, ATTN_GROUP * BLOCK), imap)
    return pl.pallas_call(
        _attn_kernel,
        out_shape=jax.ShapeDtypeStruct((b, l, ATTN_WIDTH), BF16),
        grid=(b, steps),
        in_specs=[
            pl.BlockSpec((1, ATTN_WIDTH, qb * BLOCK), lambda bi, n: (bi, 0, n)),
            k_halo(before), pl.BlockSpec((1, qb * BLOCK, KV_DIM), lambda bi, n: (bi, n, 0)), k_halo(after),
            vt_halo(before), pl.BlockSpec((1, KV_DIM, qb * BLOCK), lambda bi, n: (bi, 0, n)), vt_halo(after),
            bias_spec(first), bias_spec(mid), bias_spec(last),
            pl.BlockSpec((ATTN_KV_HEADS, 8, ATTN_GROUP * BLOCK), lambda bi, n: (0, 0, 0)),
        ],
        out_specs=pl.BlockSpec((1, qb * BLOCK, ATTN_WIDTH), lambda bi, n: (bi, n, 0)),
        compiler_params=pltpu.CompilerParams(
            dimension_semantics=("arbitrary", "arbitrary"), vmem_limit_bytes=_vmem_limit(40 << 20)),
        name="swa_attention",
    )(qt, k, k, k, vt, vt, vt, bias, bias, bias, sink)


def _attn_tables(attn_sink):
    slopes = 2.0 ** (-8.0 * jnp.arange(1, ATTN_HEADS + 1, dtype=F32) / ATTN_HEADS)
    kj = jnp.arange(3 * BLOCK)[:, None]
    qi = jnp.arange(BLOCK)[None, :]
    dist = jnp.abs(qi - kj + BLOCK)
    bias = jnp.where(dist <= WINDOW, -slopes[:, None, None] * dist.astype(F32)[None], NEG_BIG)
    bias = bias.reshape(ATTN_KV_HEADS, ATTN_GROUP, 3 * BLOCK, BLOCK).transpose(0, 2, 1, 3)
    bias = bias.reshape(ATTN_KV_HEADS, 3 * BLOCK, ATTN_GROUP * BLOCK)
    no_prev = (kj < BLOCK)[None]
    no_next = (kj >= 2 * BLOCK)[None]
    bias = jnp.stack([jnp.where(hidden, NEG_BIG, bias)
                      for hidden in (jnp.zeros_like(no_prev), no_prev, no_next, no_prev | no_next)])
    sink = jnp.broadcast_to(attn_sink.astype(F32).reshape(ATTN_KV_HEADS, 1, ATTN_GROUP, 1),
                            (ATTN_KV_HEADS, 8, ATTN_GROUP, BLOCK))
    return bias, sink.reshape(ATTN_KV_HEADS, 8, ATTN_GROUP * BLOCK)


def _dn_scan_kernel(qf_ref, kf_ref, vf_ref, qb_ref, kb_ref, vb_ref,
                    gcolf_ref, gctf_ref, gcolb_ref, gctb_ref,
                    of_ref, ob_ref, state_ref):
    @pl.when(pl.program_id(1) == 0)
    def _():
        state_ref[...] = jnp.zeros_like(state_ref)

    n = SCAN_BLOCK
    row = lax.broadcasted_iota(jnp.int32, (n, n), 0)
    col = lax.broadcasted_iota(jnp.int32, (n, n), 1)
    same_chunk = (row // CHUNK) == (col // CHUNK)
    eye = (row == col).astype(F32)
    off_diag = row != col
    directions = (
        (qf_ref, kf_ref, vf_ref, gcolf_ref, gctf_ref, of_ref, same_chunk & (row >= col)),
        (qb_ref, kb_ref, vb_ref, gcolb_ref, gctb_ref, ob_ref, same_chunk & (row <= col)),
    )
    probs = [(d, h, b) for b in range(SCAN_STEP_BLOCKS) for d in range(2) for h in range(DN_HEADS)]
    kk, qq, vv, beta, gcol, egc, p, y, attn = ({} for _ in range(9))
    for pr in probs:
        d, h, b = pr
        q_ref, k_ref, v_ref, gcol_ref, gct_ref, _, incl = directions[d]
        lane = d * DN_HEADS + h
        blk = slice(b * n, (b + 1) * n)
        kk[pr] = k_ref[0, h, blk]
        qq[pr] = q_ref[0, h, blk]
        vv[pr] = v_ref[0, h, blk]
        beta[pr] = gcol_ref[0, blk][:, lane:lane + 1]
        gcol[pr] = gcol_ref[0, blk][:, 2 * DN_HEADS + lane:2 * DN_HEADS + lane + 1]
        grow = gct_ref[0, b][lane:lane + 1, :]
        gram = lax.dot_general(jnp.concatenate([kk[pr], qq[pr]], axis=0), kk[pr],
                               (((1,), (1,)), ((), ())), preferred_element_type=F32)
        decay = jnp.exp(jnp.where(incl, gcol[pr] - grow, NEG_BIG))
        neg_n = -(beta[pr] * gram[:n] * jnp.where(off_diag, decay, 0.0))
        attn[pr] = (gram[n:] * decay).astype(BF16)
        p[pr] = eye + neg_n
        y[pr] = neg_n.astype(BF16)
    for pr in probs:
        y[pr] = _dot(y[pr], y[pr])
    for _ in range(4):
        for pr in probs:
            yb = y[pr].astype(BF16)
            r = _dot(yb, jnp.concatenate([p[pr].astype(BF16), yb], axis=1))
            p[pr] = p[pr] + r[:, :n]
            y[pr] = r[:, n:]
    for pr in probs:
        p[pr] = p[pr] + _dot(y[pr].astype(BF16), p[pr].astype(BF16))
    uwb, auw, q_eff = {}, {}, {}
    for pr in probs:
        egc[pr] = jnp.exp(gcol[pr])
        rhs = jnp.concatenate([vv[pr].astype(F32) * beta[pr],
                               kk[pr].astype(F32) * (beta[pr] * egc[pr])], axis=1)
        uwb[pr] = _dot(p[pr].astype(BF16), rhs.astype(BF16)).astype(BF16)
    for pr in probs:
        auw[pr] = _dot(attn[pr], uwb[pr])
        q_eff[pr] = qq[pr].astype(F32) * egc[pr] - auw[pr][:, n:]
    chunks_per_block = n // CHUNK
    steps = SCAN_STEP_BLOCKS * chunks_per_block

    def locate(d, step):
        pos = step if d == 0 else steps - 1 - step
        return divmod(pos, chunks_per_block)

    bm, g_last = {}, {}
    for step in range(steps):
        for d in range(2):
            for h in range(DN_HEADS):
                b, c = locate(d, step)
                pr = (d, h, b)
                rs = slice(c * CHUNK, (c + 1) * CHUNK)
                end = c * CHUNK + (CHUNK - 1 if d == 0 else 0)
                g_last[pr, c] = gcol[pr][end:end + 1]
                k_dec = kk[pr][rs].astype(F32) * jnp.exp(g_last[pr, c] - gcol[pr][rs])
                bm[pr, c] = lax.dot_general(k_dec.astype(BF16), uwb[pr][rs], (((0,), (0,)), ((), ())),
                                            preferred_element_type=F32)
    for step in range(steps):
        for d in range(2):
            o_ref = directions[d][5]
            for h in range(DN_HEADS):
                b, c = locate(d, step)
                pr = (d, h, b)
                rs = slice(c * CHUNK, (c + 1) * CHUNK)
                lane = d * DN_HEADS + h
                s_old = state_ref[lane]
                lhs = jnp.concatenate([bm[pr, c][:, n:], q_eff[pr][rs]], axis=0).astype(BF16)
                ms = _dot(lhs, s_old.astype(BF16))
                state_ref[lane] = jnp.exp(g_last[pr, c]) * s_old - ms[:n] + bm[pr, c][:, :n]
                o_ref[0, b * n + c * CHUNK:b * n + (c + 1) * CHUNK, h * DN_HEAD_DIM:(h + 1) * DN_HEAD_DIM] = (
                    ms[n:] + auw[pr][rs, :n]).astype(BF16)


def _dn_scan(q, k, v, gcol, gct):
    b, _, l, _ = q.shape
    span = SCAN_STEP_BLOCKS * SCAN_BLOCK
    nb = l // span
    fwd4 = lambda bi, i: (bi, 0, i, 0)
    bwd4 = lambda bi, i: (bi, 0, nb - 1 - i, 0)
    fwd3 = lambda bi, i: (bi, i, 0)
    bwd3 = lambda bi, i: (bi, nb - 1 - i, 0)
    fwdt = lambda bi, i: (bi, i, 0, 0)
    bwdt = lambda bi, i: (bi, nb - 1 - i, 0, 0)
    qkv_spec = lambda imap: pl.BlockSpec((1, DN_HEADS, span, DN_HEAD_DIM), imap)
    gate_spec = lambda imap: pl.BlockSpec((1, span, V7X_LANES), imap)
    gct_spec = lambda imap: pl.BlockSpec((1, SCAN_STEP_BLOCKS, 2 * DN_HEADS, SCAN_BLOCK), imap)
    out = jax.ShapeDtypeStruct((b, l, DN_WIDTH), BF16)
    return pl.pallas_call(
        _dn_scan_kernel,
        out_shape=[out, out],
        grid=(b, nb),
        in_specs=[qkv_spec(fwd4), qkv_spec(fwd4), qkv_spec(fwd4),
                  qkv_spec(bwd4), qkv_spec(bwd4), qkv_spec(bwd4),
                  gate_spec(fwd3), gct_spec(fwdt), gate_spec(bwd3), gct_spec(bwdt)],
        out_specs=[pl.BlockSpec((1, span, DN_WIDTH), fwd3),
                   pl.BlockSpec((1, span, DN_WIDTH), bwd3)],
        scratch_shapes=[pltpu.VMEM((2 * DN_HEADS, DN_HEAD_DIM, DN_HEAD_DIM), F32)],
        compiler_params=pltpu.CompilerParams(
            dimension_semantics=("arbitrary", "arbitrary"), vmem_limit_bytes=_vmem_limit(56 << 20)),
        name="dn_scan",
    )(q, k, v, q, k, v, gcol, gct, gcol, gct)


def _mix_ffn_kernel(alpha, oa_ref, of_ref, ob_ref, z_ref, x_ref, w_ref, ng_ref, g2_ref, b2_ref,
                    win_ref, wout_ref, g3_ref, b3_ref, o_ref, dn_ref, act_ref):
    parts = [slice(i * FFN_PART, (i + 1) * FFN_PART) for i in range(FFN_TILE // FFN_PART)]
    mixed = []
    for rs in parts:
        for h in range(DN_HEADS):
            sl = slice(h * DN_HEAD_DIM, (h + 1) * DN_HEAD_DIM)
            o = of_ref[rs, sl].astype(F32) + ob_ref[rs, sl].astype(F32)
            ms = jnp.mean(o * o, axis=-1, keepdims=True)
            gate = _silu(z_ref[rs, sl].astype(F32))
            dn_ref[rs, sl] = (o * lax.rsqrt(ms + RMS_EPS) * ng_ref[...] * gate).astype(BF16)
        y = _dot(oa_ref[rs, :], w_ref[:ATTN_WIDTH, :]) + _dot(dn_ref[rs, :], w_ref[ATTN_WIDTH:, :])
        mixed.append(_layer_norm(alpha * x_ref[rs, :] + y, g2_ref[...], b2_ref[...]))
    for rs, x in zip(parts, mixed):
        y = _swiglu(x.astype(BF16), win_ref, wout_ref, act_ref, rs)
        o_ref[rs, :] = _layer_norm(alpha * x + 0.5 * y, g3_ref[...], b3_ref[...])


def _mix_ffn(oa, of, ob, z, x2, w_out, norm_gain, gain2, bias2, w_in, w_ffn_out, gain3, bias3, alpha):
    rows = x2.shape[0]
    row = lambda i: (i, 0)
    const = lambda i: (0, 0)
    resident = lambda shape: pl.BlockSpec(shape, const, pipeline_mode=pl.Buffered(1))
    return pl.pallas_call(
        functools.partial(_mix_ffn_kernel, alpha),
        out_shape=jax.ShapeDtypeStruct((rows, D_MODEL), F32),
        grid=(rows // FFN_TILE,),
        in_specs=[
            pl.BlockSpec((FFN_TILE, ATTN_WIDTH), row),
            pl.BlockSpec((FFN_TILE, DN_WIDTH), row),
            pl.BlockSpec((FFN_TILE, DN_WIDTH), row),
            pl.BlockSpec((FFN_TILE, DN_WIDTH), row),
            pl.BlockSpec((FFN_TILE, D_MODEL), row),
            resident((D_MODEL, D_MODEL)),
            pl.BlockSpec((1, DN_HEAD_DIM), const),
            pl.BlockSpec((1, D_MODEL), const),
            pl.BlockSpec((1, D_MODEL), const),
            resident((D_MODEL, 2 * D_FF)),
            resident((D_FF, D_MODEL)),
            pl.BlockSpec((1, D_MODEL), const),
            pl.BlockSpec((1, D_MODEL), const),
        ],
        out_specs=pl.BlockSpec((FFN_TILE, D_MODEL), row),
        scratch_shapes=[pltpu.VMEM((FFN_TILE, DN_WIDTH), BF16), pltpu.VMEM((FFN_TILE, D_FF), BF16)],
        compiler_params=pltpu.CompilerParams(
            dimension_semantics=("arbitrary",), vmem_limit_bytes=_vmem_limit(60 << 20)),
        name="mix_ffn",
    )(oa, of, ob, z, x2, w_out, norm_gain, gain2, bias2, w_in, w_ffn_out, gain3, bias3)


def _pack_mixer_weights(w_in):
    aq = w_in[:, :ATTN_WIDTH]
    ak = w_in[:, ATTN_WIDTH:ATTN_WIDTH + KV_DIM]
    av = w_in[:, ATTN_WIDTH + KV_DIM:ATTN_WIDTH + 2 * KV_DIM]
    dqkv = w_in[:, ATTN_WIDTH + 2 * KV_DIM:ATTN_WIDTH + 2 * KV_DIM + 3 * DN_WIDTH]
    z = w_in[:, ATTN_WIDTH + 2 * KV_DIM + 3 * DN_WIDTH:OFF_B]
    gates = w_in[:, OFF_B:]
    w_row = jnp.concatenate([ak, z, dqkv], axis=1).astype(BF16)
    w_t = jnp.concatenate([aq, av, gates], axis=1).T.astype(BF16)
    return w_row, w_t


def _gate_params(a_log, dt_bias):
    cols = jnp.stack([a_log.reshape(-1), dt_bias.reshape(-1)], axis=1).astype(F32)
    return jnp.pad(cols, ((GATE_ROWS - cols.shape[0], 0), (0, V7X_LANES - cols.shape[1])))


def kernel(x_prompt, x_sample, ffn1_w_in, ffn1_w_out, w_in, conv_w, attn_sink, dn_a_log, dn_dt_bias,
           dn_norm_gain, w_out, ffn2_w_in, ffn2_w_out, ln_gain, ln_bias):
    depth = w_in.shape[0]
    alpha = (2.0 * depth) ** 0.25
    tri = _chunk_triangles()
    layers = []
    for i in range(depth):
        bias, sink = _attn_tables(attn_sink[i])
        w_row, w_t = _pack_mixer_weights(w_in[i])
        layers.append(dict(
            ffn1_in=ffn1_w_in[i].astype(BF16), ffn1_out=ffn1_w_out[i].astype(BF16),
            ffn2_in=ffn2_w_in[i].astype(BF16), ffn2_out=ffn2_w_out[i].astype(BF16),
            w_row=w_row, w_t=w_t, w_out=w_out[i].astype(BF16),
            conv=jnp.pad(conv_w[i].astype(F32), ((0, 8 - CONV_WIDTH), (0, 0))),
            gate=_gate_params(dn_a_log[i], dn_dt_bias[i]),
            norm_gain=dn_norm_gain[i].astype(F32).reshape(1, DN_HEAD_DIM),
            bias=bias, sink=sink,
            ln_g=ln_gain[i].astype(F32).reshape(3, 1, D_MODEL),
            ln_b=ln_bias[i].astype(F32).reshape(3, 1, D_MODEL),
        ))

    def trunk(x):
        b, l, _ = x.shape
        x2 = x.reshape(b * l, D_MODEL)
        for p in layers:
            x2 = _ffn_ln(x2, p["ffn1_in"], p["ffn1_out"], p["ln_g"][0], p["ln_b"][0], alpha)
            qt, ak, vt, z, dq, dk, dv, gcol, gct = _mixer_in(
                x2.reshape(b, l, D_MODEL), p["w_row"], p["w_t"], p["conv"], p["gate"], tri)
            oa = _attention(qt, ak, vt, p["bias"], p["sink"])
            of, ob = _dn_scan(dq, dk, dv, gcol, gct)
            x2 = _mix_ffn(oa.reshape(b * l, ATTN_WIDTH), of.reshape(b * l, DN_WIDTH),
                          ob.reshape(b * l, DN_WIDTH), z.reshape(b * l, DN_WIDTH), x2, p["w_out"],
                          p["norm_gain"], p["ln_g"][1], p["ln_b"][1],
                          p["ffn2_in"], p["ffn2_out"], p["ln_g"][2], p["ln_b"][2], alpha)
        return x2.reshape(b, l, D_MODEL)

    return (trunk(x_prompt), trunk(x_sample))
```
